```python
import jax, jax.numpy as jnp
from jax import lax
import numpy as np

D_MODEL = 2048
BATCH = 4
SEQ = 4096
DEPTH = 2

N_EVEN = (DEPTH + 1) // 2
N_ODD = DEPTH // 2
NORM_EPS = 1e-6

D_FF = 5632
MACARON_WEIGHT = 0.5

GLA_HEADS = 4
GLA_DK = D_MODEL // 16
GLA_DV = D_MODEL // 8
GLA_GATE_RANK = 16
GLA_TAU = 16.0
GLA_CHUNK = 64

SWA_HEAD_DIM = 64
SWA_Q_HEADS = D_MODEL // 128
SWA_KV_HEADS = 2
SWA_WINDOW = 128
SWA_BLOCK = 128
ROPE_THETA = 500000.0
ROPE_DIM = SWA_HEAD_DIM // 4

GLA_QK_W = GLA_HEADS * GLA_DK
GLA_V_W = GLA_HEADS * GLA_DV
SWA_Q_W = SWA_Q_HEADS * SWA_HEAD_DIM
SWA_KV_W = SWA_KV_HEADS * SWA_HEAD_DIM
HYB_SPLITS = (GLA_QK_W, GLA_QK_W, GLA_V_W, GLA_V_W, GLA_GATE_RANK, SWA_Q_W, SWA_KV_W, SWA_KV_W)
HYB_IN = GLA_QK_W * 2 + GLA_V_W * 2 + GLA_GATE_RANK + SWA_Q_W + SWA_KV_W * 2
HYB_OUT = GLA_V_W + SWA_Q_W

SSD_D_INNER = 2 * D_MODEL
SSD_HEAD_DIM = 64
SSD_HEADS = SSD_D_INNER // SSD_HEAD_DIM
SSD_GROUPS = 8
SSD_D_STATE = 128
SSD_CONV = 4
SSD_CHUNK = 64
SSD_CONV_CH = SSD_D_INNER + 2 * SSD_GROUPS * SSD_D_STATE
SSD_IN = SSD_D_INNER + SSD_CONV_CH + SSD_HEADS

kernel_name = 'hybrid_gla_swa_ssd_macaron'


def rms_norm(x, gain, eps=NORM_EPS):
    xf = x.astype(jnp.float32)
    y = xf * lax.rsqrt(jnp.mean(xf * xf, axis=-1, keepdims=True) + eps)
    return (y * gain.astype(jnp.float32)).astype(x.dtype)


def swiglu(h, w_gu, w_down):
    g, u = jnp.split(h @ w_gu, 2, axis=-1)
    return (jax.nn.silu(g) * u) @ w_down


def partial_rope(x, positions):
    half = ROPE_DIM // 2
    inv_freq = ROPE_THETA ** (-2.0 * jnp.arange(half, dtype=jnp.float32) / ROPE_DIM)
    ang = positions.astype(jnp.float32)[..., None] * inv_freq
    cos = jnp.cos(ang)[:, :, None, :]
    sin = jnp.sin(ang)[:, :, None, :]
    xf = x.astype(jnp.float32)
    x1, x2, rest = xf[..., :half], xf[..., half:ROPE_DIM], xf[..., ROPE_DIM:]
    out = jnp.concatenate([x1 * cos - x2 * sin, x2 * cos + x1 * sin, rest], axis=-1)
    return out.astype(x.dtype)


def gla_chunked(q, k, v, log_a):
    b_, s_, nh, dk = q.shape
    dv = v.shape[-1]
    c = GLA_CHUNK
    nc = s_ // c
    rs = lambda t: t.reshape(b_, nc, c, nh, t.shape[-1])
    q, k, v, log_a = rs(q), rs(k), rs(v), rs(log_a)
    bcum = jnp.cumsum(log_a, axis=2)
    q_in = q * jnp.exp(bcum)
    k_in = k * jnp.exp(-bcum)
    causal = jnp.tril(jnp.ones((c, c), bool))
    attn = jnp.where(causal, jnp.einsum('bcihd,bcjhd->bchij', q_in, k_in), 0.0)
    o_intra = jnp.einsum('bchij,bcjhv->bcihv', attn, v)
    b_last = bcum[:, :, -1:]
    k_end = k * jnp.exp(b_last - bcum)
    decay_chunk = jnp.exp(b_last[:, :, 0])

    def step(state, inp):
        q_c, k_c, v_c, d_c = inp
        o_c = jnp.einsum('bihd,bhdv->bihv', q_c, state)
        state = d_c[..., None] * state + jnp.einsum('bjhd,bjhv->bhdv', k_c, v_c)
        return state, o_c

    state0 = jnp.zeros((b_, nh, dk, dv), jnp.float32)
    xs = (jnp.moveaxis(q_in, 1, 0), jnp.moveaxis(k_end, 1, 0), jnp.moveaxis(v, 1, 0), jnp.moveaxis(decay_chunk, 1, 0))
    _, o_inter = lax.scan(step, state0, xs)
    o = o_intra + jnp.moveaxis(o_inter, 0, 1)
    return o.reshape(b_, s_, nh, dv)


def swa_sink_attention(q, k, v, sinks):
    b_, s_, hq, dh = q.shape
    hkv = k.shape[2]
    g = hq // hkv
    w = SWA_BLOCK
    nb = s_ // w
    qb = q.reshape(b_, nb, w, hkv, g, dh)

    def band(t):
        prev = jnp.pad(t, ((0, 0), (w, 0), (0, 0), (0, 0)))[:, :s_]
        return jnp.concatenate([prev.reshape(b_, nb, w, hkv, dh), t.reshape(b_, nb, w, hkv, dh)], axis=2)

    kb, vb = band(k), band(v)
    scores = jnp.einsum('bnqkgd,bnskd->bnkgqs', qb, kb).astype(jnp.float32) * (dh ** -0.5)
    rel = (jnp.arange(w)[:, None] + w) - jnp.arange(2 * w)[None, :]
    key_abs = jnp.arange(nb)[:, None, None] * w - w + jnp.arange(2 * w)[None, None, :]
    valid = (rel >= 0) & (rel < SWA_WINDOW) & (key_abs >= 0)
    scores = jnp.where(valid[None, :, None, None], scores, -1e30)
    sink = sinks.astype(jnp.float32).reshape(hkv, g)[None, None, :, :, None]
    m = jnp.maximum(scores.max(axis=-1), sink)
    p = jnp.exp(scores - m[..., None])
    denom = p.sum(axis=-1) + jnp.exp(sink - m)
    p = p / denom[..., None]
    o = jnp.einsum('bnkgqs,bnskd->bnqkgd', p, vb.astype(jnp.float32))
    return o.reshape(b_, s_, hq * dh).astype(q.dtype)


def gla_swa_mixer(h, positions, w_in, gate_w2, gate_b, gla_norm, q_norm, k_norm, sinks, w_out):
    b_, s_, _ = h.shape
    f32 = jnp.float32
    idx = np.cumsum(HYB_SPLITS)[:-1].tolist()
    gq, gk, gv, gr, glr, sq, sk, sv = jnp.split(h @ w_in, idx, axis=-1)
    hd = lambda t, n: t.reshape(b_, s_, n, -1)
    q = hd(gq, GLA_HEADS).astype(f32) * (GLA_DK ** -0.5)
    k = hd(gk, GLA_HEADS).astype(f32)
    v = hd(gv, GLA_HEADS).astype(f32)
    log_a = jax.nn.log_sigmoid((glr @ gate_w2 + gate_b).astype(f32)) / GLA_TAU
    o = gla_chunked(q, k, v, hd(log_a, GLA_HEADS))
    o = rms_norm(o, gla_norm) * jax.nn.silu(hd(gr, GLA_HEADS).astype(f32))
    o_gla = o.reshape(b_, s_, GLA_V_W).astype(h.dtype)
    qs = partial_rope(rms_norm(hd(sq, SWA_Q_HEADS), q_norm), positions)
    ks = partial_rope(rms_norm(hd(sk, SWA_KV_HEADS), k_norm), positions)
    vs = hd(sv, SWA_KV_HEADS)
    o_swa = swa_sink_attention(qs, ks, vs, sinks)
    return jnp.concatenate([o_gla, o_swa], axis=-1) @ w_out


def ssd_mixer(h, w_in, conv_w, conv_b, dt_bias, a_log, d_skip, norm_g, w_out):
    b_, s_, _ = h.shape
    f32 = jnp.float32
    z, xbc, dt = jnp.split(h @ w_in, [SSD_D_INNER, SSD_D_INNER + SSD_CONV_CH], axis=-1)
    xbc = lax.conv_general_dilated(xbc, conv_w.astype(xbc.dtype)[:, None, :], window_strides=(1,),
                                   padding=[(SSD_CONV - 1, 0)], dimension_numbers=('NWC', 'WIO', 'NWC'),
                                   feature_group_count=SSD_CONV_CH) + conv_b
    xbc = jax.nn.silu(xbc).astype(f32)
    xm, bm, cm = jnp.split(xbc, [SSD_D_INNER, SSD_D_INNER + SSD_GROUPS * SSD_D_STATE], axis=-1)
    hg = SSD_HEADS // SSD_GROUPS
    L = SSD_CHUNK
    nc = s_ // L
    x = xm.reshape(b_, nc, L, SSD_GROUPS, hg, SSD_HEAD_DIM)
    bm = bm.reshape(b_, nc, L, SSD_GROUPS, SSD_D_STATE)
    cm = cm.reshape(b_, nc, L, SSD_GROUPS, SSD_D_STATE)
    dt = jax.nn.softplus(dt.astype(f32) + dt_bias.astype(f32)).reshape(b_, nc, L, SSD_GROUPS, hg)
    a = -jnp.exp(a_log.astype(f32)).reshape(SSD_GROUPS, hg)
    a_cum = jnp.cumsum(dt * a, axis=2)
    xdt = x * dt[..., None]
    causal = jnp.tril(jnp.ones((L, L), bool))[:, :, None, None]
    seg = a_cum[:, :, :, None] - a_cum[:, :, None, :]
    decay = jnp.exp(jnp.where(causal, seg, -jnp.inf))
    cb = jnp.einsum('bclgn,bcsgn->bclsg', cm, bm)
    y_diag = jnp.einsum('bclsg,bclsgk,bcsgkp->bclgkp', cb, decay, xdt)

    def step(state, inp):
        c_c, b_c, x_c, a_c = inp
        y_c = jnp.einsum('blgn,bgkpn->blgkp', c_c, state) * jnp.exp(a_c)[..., None]
        a_end = a_c[:, -1]
        wgt = jnp.exp(a_end[:, None] - a_c)
        state = jnp.exp(a_end)[..., None, None] * state + jnp.einsum('blgn,blgk,blgkp->bgkpn', b_c, wgt, x_c)
        return state, y_c

    state0 = jnp.zeros((b_, SSD_GROUPS, hg, SSD_HEAD_DIM, SSD_D_STATE), f32)
    xs = (jnp.moveaxis(cm, 1, 0), jnp.moveaxis(bm, 1, 0), jnp.moveaxis(xdt, 1, 0), jnp.moveaxis(a_cum, 1, 0))
    _, y_off = lax.scan(step, state0, xs)
    y = y_diag + jnp.moveaxis(y_off, 0, 1) + x * d_skip.astype(f32).reshape(SSD_GROUPS, hg)[..., None]
    y = y.reshape(b_, s_, SSD_D_INNER) * jax.nn.silu(z.astype(f32))
    y = rms_norm(y.reshape(b_, s_, SSD_GROUPS, -1), norm_g.reshape(SSD_GROUPS, -1)).reshape(b_, s_, SSD_D_INNER)
    return y.astype(h.dtype) @ w_out


def setup_inputs(seed: int = 0) -> dict:
    key = jax.random.key(seed)
    ks = jax.random.split(key, 24)
    nrm = lambda k, shape, scale: jax.random.normal(k, shape, jnp.float32) * scale
    gain = lambda k, shape: 1.0 + 0.02 * jax.random.normal(k, shape, jnp.float32)
    x = jax.random.normal(ks[0], (BATCH, SEQ, D_MODEL), jnp.float32)
    offset = jax.random.randint(ks[1], (BATCH, 1), 0, SEQ, dtype=jnp.int32)
    positions = offset + jnp.arange(SEQ, dtype=jnp.int32)[None, :]
    dt0 = jnp.exp(jax.random.uniform(ks[19], (N_ODD, SSD_HEADS), jnp.float32, np.log(1e-3), np.log(1e-1)))
    return {
        'x': x,
        'positions': positions,
        'norm_ffn': gain(ks[2], (DEPTH, 2, D_MODEL)),
        'w_ffn_gu': nrm(ks[3], (DEPTH, 2, D_MODEL, 2 * D_FF), D_MODEL ** -0.5),
        'w_ffn_down': nrm(ks[4], (DEPTH, 2, D_FF, D_MODEL), D_FF ** -0.5),
        'norm_mix': gain(ks[5], (DEPTH, D_MODEL)),
        'hyb_w_in': nrm(ks[6], (N_EVEN, D_MODEL, HYB_IN), D_MODEL ** -0.5),
        'gla_gate_w2': nrm(ks[7], (N_EVEN, GLA_GATE_RANK, GLA_QK_W), GLA_GATE_RANK ** -0.5),
        'gla_gate_b': nrm(ks[8], (N_EVEN, GLA_QK_W), 0.1),
        'gla_norm': gain(ks[9], (N_EVEN, GLA_DV)),
        'attn_q_norm': gain(ks[10], (N_EVEN, SWA_HEAD_DIM)),
        'attn_k_norm': gain(ks[11], (N_EVEN, SWA_HEAD_DIM)),
        'attn_sinks': nrm(ks[12], (N_EVEN, SWA_Q_HEADS), 0.5),
        'hyb_w_out': nrm(ks[13], (N_EVEN, HYB_OUT, D_MODEL), HYB_OUT ** -0.5),
        'ssd_w_in': nrm(ks[14], (N_ODD, D_MODEL, SSD_IN), D_MODEL ** -0.5),
        'ssd_conv_w': nrm(ks[15], (N_ODD, SSD_CONV, SSD_CONV_CH), SSD_CONV ** -0.5),
        'ssd_conv_b': nrm(ks[16], (N_ODD, SSD_CONV_CH), 0.02),
        'ssd_dt_bias': dt0 + jnp.log(-jnp.expm1(-dt0)),
        'ssd_a_log': jnp.log(jax.random.uniform(ks[17], (N_ODD, SSD_HEADS), jnp.float32, 1.0, 16.0)),
        'ssd_d': gain(ks[18], (N_ODD, SSD_HEADS)),
        'ssd_norm': gain(ks[20], (N_ODD, SSD_D_INNER)),
        'ssd_w_out': nrm(ks[21], (N_ODD, SSD_D_INNER, D_MODEL), SSD_D_INNER ** -0.5),
    }


def reference(x, positions, norm_ffn, w_ffn_gu, w_ffn_down, norm_mix, hyb_w_in, gla_gate_w2, gla_gate_b,
              gla_norm, attn_q_norm, attn_k_norm, attn_sinks, hyb_w_out, ssd_w_in, ssd_conv_w, ssd_conv_b,
              ssd_dt_bias, ssd_a_log, ssd_d, ssd_norm, ssd_w_out):
    h = x
    for layer in range(DEPTH):
        h = h + MACARON_WEIGHT * swiglu(rms_norm(h, norm_ffn[layer, 0]), w_ffn_gu[layer, 0], w_ffn_down[layer, 0])
        hn = rms_norm(h, norm_mix[layer])
        i = layer // 2
        if layer % 2 == 0:
            h = h + gla_swa_mixer(hn, positions, hyb_w_in[i], gla_gate_w2[i], gla_gate_b[i], gla_norm[i],
                                  attn_q_norm[i], attn_k_norm[i], attn_sinks[i], hyb_w_out[i])
        else:
            h = h + ssd_mixer(hn, ssd_w_in[i], ssd_conv_w[i], ssd_conv_b[i], ssd_dt_bias[i], ssd_a_log[i],
                              ssd_d[i], ssd_norm[i], ssd_w_out[i])
        h = h + MACARON_WEIGHT * swiglu(rms_norm(h, norm_ffn[layer, 1]), w_ffn_gu[layer, 1], w_ffn_down[layer, 1])
    return h
```

```python
import functools

import numpy as np
import jax
import jax.numpy as jnp
from jax import lax
from jax.experimental import pallas as pl
from jax.experimental.pallas import tpu as pltpu

F32 = jnp.float32
BF16 = jnp.bfloat16

D_MODEL = 2048
D_FF = 5632
NORM_EPS = 1e-6
MACARON_WEIGHT = 0.5

GLA_HEADS = 4
GLA_DK = 128
GLA_DV = 256
GLA_GATE_RANK = 16
GLA_TAU = 16.0
GLA_CHUNK = 64

SWA_HEAD_DIM = 64
SWA_Q_HEADS = 16
SWA_KV_HEADS = 2
SWA_WINDOW = 128
SWA_BLOCK = 128
ROPE_THETA = 500000.0
ROPE_DIM = 16

SSD_D_INNER = 4096
SSD_HEAD_DIM = 64
SSD_HEADS = 64
SSD_GROUPS = 8
SSD_D_STATE = 128
SSD_CONV = 4
SSD_CHUNK = 64
SSD_GROUP_W = SSD_D_INNER // SSD_GROUPS
SSD_BC_W = SSD_GROUPS * SSD_D_STATE
SSD_CONV_CH = SSD_D_INNER + 2 * SSD_BC_W

LANES = 128
VMEM_LIMIT_CAP = 56 * 1024 * 1024

HYB_Q, HYB_K, HYB_V, HYB_R, HYB_SQ, HYB_SK, HYB_SV, HYB_GLR = 0, 512, 1024, 2048, 3072, 4096, 4224, 4352
HYB_W = 4608
SSD_XBC, SSD_Z, SSD_DT = 0, SSD_CONV_CH, SSD_CONV_CH + SSD_D_INNER
SSD_W = 10752


def _compiler_params(semantics, vmem_bytes):
    limit = min(int(vmem_bytes * 1.25) + (4 << 20), VMEM_LIMIT_CAP)
    return pltpu.CompilerParams(dimension_semantics=semantics, vmem_limit_bytes=limit)


def _rms_norm(x, gain):
    ms = jnp.mean(x * x, axis=-1, keepdims=True)
    return x * lax.rsqrt(ms + NORM_EPS) * gain


def _silu(x):
    return x * jax.nn.sigmoid(x)


def _softplus(x):
    return jnp.maximum(x, 0.0) + jnp.log1p(jnp.exp(-jnp.abs(x)))


def _split_bf16(x, n):
    parts, rest = [], x
    for _ in range(n):
        p = rest.astype(BF16)
        parts.append(p)
        rest = rest - p.astype(F32)
    return parts


def _dot(a, b):
    return jnp.dot(a, b, preferred_element_type=F32)


def _dot_nt(a, b):
    return lax.dot_general(a, b, (((1,), (1,)), ((), ())), preferred_element_type=F32)


def _select_dot_lhs(sel, x, n):
    return sum(_dot(sel, p) for p in _split_bf16(x, n))


def _select_dot_rhs(x, sel, n):
    return sum(_dot(p, sel) for p in _split_bf16(x, n))


def _ffn_kernel(x_ref, g_ref, wg_ref, wu_ref, wd_ref, o_ref, hn_ref, acc_ref):
    j = pl.program_id(1)

    @pl.when(j == 0)
    def _():
        hn_ref[...] = _rms_norm(x_ref[...], g_ref[...]).astype(BF16)
        acc_ref[...] = jnp.zeros_like(acc_ref)

    h = hn_ref[...]
    gate = _dot(h, wg_ref[...])
    up = _dot(h, wu_ref[...])
    act = (_silu(gate) * up).astype(BF16)
    acc_ref[...] += _dot(act, wd_ref[...])

    @pl.when(j == pl.num_programs(1) - 1)
    def _():
        o_ref[...] = x_ref[...] + MACARON_WEIGHT * acc_ref[...]


def _ffn(h, gain, w_gu, w_down, tm=512, tf=512):
    t, d = h.shape
    nf = D_FF // tf
    vmem = 2 * (2 * tm * d * 4) + 2 * 3 * d * tf * 2 + tm * d * (2 + 4) + 3 * tm * tf * 4
    return pl.pallas_call(
        _ffn_kernel,
        grid=(t // tm, nf),
        in_specs=[
            pl.BlockSpec((tm, d), lambda i, j: (i, 0)),
            pl.BlockSpec((1, d), lambda i, j: (0, 0)),
            pl.BlockSpec((d, tf), lambda i, j: (0, j)),
            pl.BlockSpec((d, tf), lambda i, j: (0, j + nf)),
            pl.BlockSpec((tf, d), lambda i, j: (j, 0)),
        ],
        out_specs=pl.BlockSpec((tm, d), lambda i, j: (i, 0)),
        out_shape=jax.ShapeDtypeStruct((t, d), F32),
        scratch_shapes=[pltpu.VMEM((tm, d), BF16), pltpu.VMEM((tm, d), F32)],
        compiler_params=_compiler_params(("parallel", "arbitrary"), vmem),
        name="ffn",
    )(h, gain.reshape(1, d), w_gu, w_gu, w_down)


def _norm_matmul_kernel(x_ref, g_ref, w_ref, o_ref, hn_ref):
    @pl.when(pl.program_id(1) == 0)
    def _():
        hn_ref[...] = _rms_norm(x_ref[...], g_ref[...]).astype(BF16)

    o_ref[...] = _dot(hn_ref[...], w_ref[...]).astype(o_ref.dtype)


def _norm_matmul(h, gain, w, tn, tm=512):
    t, d = h.shape
    n = w.shape[1]
    vmem = 2 * tm * d * 4 + 2 * d * tn * 2 + 2 * tm * tn * 4 + tm * d * 2
    return pl.pallas_call(
        _norm_matmul_kernel,
        grid=(t // tm, n // tn),
        in_specs=[
            pl.BlockSpec((tm, d), lambda i, j: (i, 0)),
            pl.BlockSpec((1, d), lambda i, j: (0, 0)),
            pl.BlockSpec((d, tn), lambda i, j: (0, j)),
        ],
        out_specs=pl.BlockSpec((tm, tn), lambda i, j: (i, j)),
        out_shape=jax.ShapeDtypeStruct((t, n), F32),
        scratch_shapes=[pltpu.VMEM((tm, d), BF16)],
        compiler_params=_compiler_params(("parallel", "arbitrary"), vmem),
        name="norm_matmul",
    )(h, gain.reshape(1, d), w)


def _matmul_residual_kernel(*refs, n_in):
    a_refs, w_refs = refs[:n_in], refs[n_in:2 * n_in]
    r_ref, o_ref = refs[2 * n_in], refs[2 * n_in + 1]
    acc = r_ref[...]
    for a_ref, w_ref in zip(a_refs, w_refs):
        acc = acc + _dot(a_ref[...], w_ref[...])
    o_ref[...] = acc


def _matmul_residual(acts, weights, res, tm=512):
    t, d = res.shape
    n_in = len(acts)
    vmem = 2 * 2 * tm * d * 4
    in_specs = []
    for a in acts:
        in_specs.append(pl.BlockSpec((tm, a.shape[1]), lambda i: (i, 0)))
        vmem += 2 * tm * a.shape[1] * 2
    for w in weights:
        in_specs.append(pl.BlockSpec(w.shape, lambda i: (0, 0)))
        vmem += 2 * w.shape[0] * w.shape[1] * 2
    in_specs.append(pl.BlockSpec((tm, d), lambda i: (i, 0)))
    return pl.pallas_call(
        functools.partial(_matmul_residual_kernel, n_in=n_in),
        grid=(t // tm,),
        in_specs=in_specs,
        out_specs=pl.BlockSpec((tm, d), lambda i: (i, 0)),
        out_shape=jax.ShapeDtypeStruct((t, d), F32),
        compiler_params=_compiler_params(("parallel",), vmem),
        name="matmul_residual",
    )(*acts, *weights, res)


def _gla_kernel(q_ref, k_ref, v_ref, r_ref, glr_ref, w2_ref, gb_ref, gn_ref, o_ref, st_ref, *, n_chunks):
    @pl.when(pl.program_id(2) == 0)
    def _():
        st_ref[...] = jnp.zeros_like(st_ref)

    c = GLA_CHUNK
    ri = lax.broadcasted_iota(jnp.int32, (c, c), 0)
    ci = lax.broadcasted_iota(jnp.int32, (c, c), 1)
    causal = ri >= ci
    tril = jnp.where(causal, 1.0, 0.0).astype(BF16)
    w2, gb, gn = w2_ref[...], gb_ref[...], gn_ref[...]
    state = st_ref[...]
    for idx in range(n_chunks):
        rows = pl.ds(idx * c, c)
        z = _dot(glr_ref[rows, :].astype(BF16), w2) + gb
        log_a = (jnp.minimum(z, 0.0) - jnp.log1p(jnp.exp(-jnp.abs(z)))) * (1.0 / GLA_TAU)
        bcum = _select_dot_lhs(tril, log_a, 3)
        b_last = bcum[c - 1:c, :]
        q = q_ref[rows, :] * (GLA_DK ** -0.5)
        k = k_ref[rows, :]
        v = v_ref[rows, :]
        q_in = (q * jnp.exp(bcum)).astype(BF16)
        k_in = (k * jnp.exp(-bcum)).astype(BF16)
        k_end = (k * jnp.exp(b_last - bcum)).astype(BF16)
        attn = jnp.where(causal, _dot_nt(q_in, k_in), 0.0).astype(BF16)
        o = _dot(attn, v.astype(BF16)) + _dot_nt(q_in, state.astype(BF16))
        state = jnp.exp(b_last) * state + _dot(v.T.astype(BF16), k_end)
        o = _rms_norm(o, gn) * _silu(r_ref[rows, :])
        o_ref[rows, :] = o.astype(o_ref.dtype)
    st_ref[...] = state


def _gla(proj, w2p, gate_b, gla_norm, batch, seq, lb=256):
    t = batch * seq
    ns = seq // lb
    row = lambda b, h, n: b * ns + n
    vmem = 2 * lb * (2 * GLA_DK + 2 * GLA_DV + LANES) * 4 + 2 * lb * GLA_DV * 2 + (1 << 20)
    return pl.pallas_call(
        functools.partial(_gla_kernel, n_chunks=lb // GLA_CHUNK),
        grid=(batch, GLA_HEADS, ns),
        in_specs=[
            pl.BlockSpec((lb, GLA_DK), lambda b, h, n: (row(b, h, n), HYB_Q // GLA_DK + h)),
            pl.BlockSpec((lb, GLA_DK), lambda b, h, n: (row(b, h, n), HYB_K // GLA_DK + h)),
            pl.BlockSpec((lb, GLA_DV), lambda b, h, n: (row(b, h, n), HYB_V // GLA_DV + h)),
            pl.BlockSpec((lb, GLA_DV), lambda b, h, n: (row(b, h, n), HYB_R // GLA_DV + h)),
            pl.BlockSpec((lb, LANES), lambda b, h, n: (row(b, h, n), HYB_GLR // LANES)),
            pl.BlockSpec((LANES, GLA_DK), lambda b, h, n: (0, h)),
            pl.BlockSpec((1, GLA_DK), lambda b, h, n: (0, h)),
            pl.BlockSpec((1, GLA_DV), lambda b, h, n: (0, 0)),
        ],
        out_specs=pl.BlockSpec((lb, GLA_DV), lambda b, h, n: (row(b, h, n), h)),
        out_shape=jax.ShapeDtypeStruct((t, GLA_HEADS * GLA_DV), BF16),
        scratch_shapes=[pltpu.VMEM((GLA_DV, GLA_DK), F32)],
        compiler_params=_compiler_params(("parallel", "parallel", "arbitrary"), vmem),
        name="gla",
    )(proj, proj, proj, proj, proj, w2p, gate_b.reshape(1, -1), gla_norm.reshape(1, -1))


def _swa_kernel(sink_ref, posq_ref, posp_ref, q_ref, kvc_ref, kvp_ref, qn_ref, kn_ref, freq_ref, bd_ref, o_ref):
    n = pl.program_id(1)
    w = SWA_BLOCK
    lane = lax.broadcasted_iota(jnp.int32, (w, LANES), 1)
    lane_lo = lane < SWA_HEAD_DIM
    lane_lo_kv = lax.broadcasted_iota(jnp.int32, (2 * w, LANES), 1) < SWA_HEAD_DIM
    dim = lane & (SWA_HEAD_DIM - 1)
    half = ROPE_DIM // 2
    sin_hi = (dim >= half) & (dim < ROPE_DIM)
    sin_lo = dim < half
    freq = freq_ref[...]
    bd = bd_ref[...]

    def rope_tables(pos_ref):
        ang = pos_ref[...].astype(F32) * freq
        cos, sin = jnp.cos(ang), jnp.sin(ang)
        return cos, jnp.where(sin_hi, sin, 0.0), jnp.where(sin_lo, -sin, 0.0)

    def norm_rope(x, gain, tabs):
        ms = _select_dot_rhs(x * x, bd, 2) * (1.0 / SWA_HEAD_DIM)
        xn = x * lax.rsqrt(ms + NORM_EPS) * gain
        cos, s_hi, s_lo = tabs
        return xn * cos + pltpu.roll(xn, half, 1) * s_hi + pltpu.roll(xn, LANES - half, 1) * s_lo

    tq = rope_tables(posq_ref)
    tp = rope_tables(posp_ref)
    kn = kn_ref[...]
    kk = jnp.concatenate([norm_rope(kvp_ref[:, 0:LANES], kn, tp), norm_rope(kvc_ref[:, 0:LANES], kn, tq)], axis=0)
    vv = jnp.concatenate([kvp_ref[:, LANES:2 * LANES], kvc_ref[:, LANES:2 * LANES]], axis=0)
    kk_sw = pltpu.roll(kk, SWA_HEAD_DIM, 1)
    vv_sw = pltpu.roll(vv, SWA_HEAD_DIM, 1)
    k2 = [jnp.where(lane_lo_kv, kk, kk_sw).astype(BF16), jnp.where(lane_lo_kv, kk_sw, kk).astype(BF16)]
    v2 = [jnp.where(lane_lo_kv, vv, vv_sw).astype(BF16), jnp.where(lane_lo_kv, vv_sw, vv).astype(BF16)]

    qi = lax.broadcasted_iota(jnp.int32, (w, 2 * w), 0)
    ki = lax.broadcasted_iota(jnp.int32, (w, 2 * w), 1)
    rel = qi + w - ki
    valid = (rel >= 0) & (rel < SWA_WINDOW) & ((ki >= w) | (n > 0))

    qn = qn_ref[...]
    heads_per_kv = SWA_Q_HEADS // SWA_KV_HEADS
    for pair in range(SWA_Q_HEADS // 2):
        g = (2 * pair) // heads_per_kv
        cols = pl.ds(pair * LANES, LANES)
        qp = norm_rope(q_ref[:, cols], qn, tq)
        outs = []
        for hf in range(2):
            keep = lane_lo if hf == 0 else jnp.logical_not(lane_lo)
            qm = jnp.where(keep, qp, 0.0).astype(BF16)
            s = _dot_nt(qm, k2[g]) * (SWA_HEAD_DIM ** -0.5)
            s = jnp.where(valid, s, -1e30)
            sink = sink_ref[2 * pair + hf]
            m = jnp.maximum(jnp.max(s, axis=-1, keepdims=True), sink)
            p = jnp.exp(s - m)
            denom = jnp.sum(p, axis=-1, keepdims=True) + jnp.exp(sink - m)
            outs.append(_dot(p.astype(BF16), v2[g]) * (1.0 / denom))
        o_ref[:, cols] = jnp.where(lane_lo, outs[0], outs[1]).astype(o_ref.dtype)


def _swa(proj, pos_col, sinks, q_norm, k_norm, batch, seq):
    t = batch * seq
    w = SWA_BLOCK
    nb = seq // w
    half = ROPE_DIM // 2
    inv_freq = ROPE_THETA ** (-2.0 * jnp.arange(half, dtype=F32) / ROPE_DIM)
    dim = np.arange(LANES) % SWA_HEAD_DIM
    freq_row = jnp.where(dim < ROPE_DIM, inv_freq[dim % half], 0.0).reshape(1, LANES).astype(F32)
    blockdiag = jnp.asarray(np.kron(np.eye(LANES // SWA_HEAD_DIM), np.ones((SWA_HEAD_DIM, SWA_HEAD_DIM))), BF16)
    cur = lambda b, n: b * nb + n
    prev = lambda b, n: b * nb + jnp.maximum(n - 1, 0)
    q_w = SWA_Q_HEADS * SWA_HEAD_DIM
    vmem = 2 * w * (q_w + 2 * 2 * LANES) * 4 + 2 * w * q_w * 2 + 4 * w * LANES * 4 + (8 << 20)
    return pl.pallas_call(
        _swa_kernel,
        grid=(batch, nb),
        in_specs=[
            pl.BlockSpec(memory_space=pltpu.SMEM),
            pl.BlockSpec((w, 1), lambda b, n: (cur(b, n), 0)),
            pl.BlockSpec((w, 1), lambda b, n: (prev(b, n), 0)),
            pl.BlockSpec((w, q_w), lambda b, n: (cur(b, n), HYB_SQ // q_w)),
            pl.BlockSpec((w, 2 * LANES), lambda b, n: (cur(b, n), HYB_SK // (2 * LANES))),
            pl.BlockSpec((w, 2 * LANES), lambda b, n: (prev(b, n), HYB_SK // (2 * LANES))),
            pl.BlockSpec((1, LANES), lambda b, n: (0, 0)),
            pl.BlockSpec((1, LANES), lambda b, n: (0, 0)),
            pl.BlockSpec((1, LANES), lambda b, n: (0, 0)),
            pl.BlockSpec((LANES, LANES), lambda b, n: (0, 0)),
        ],
        out_specs=pl.BlockSpec((w, q_w), lambda b, n: (cur(b, n), 0)),
        out_shape=jax.ShapeDtypeStruct((t, q_w), BF16),
        compiler_params=_compiler_params(("parallel", "arbitrary"), vmem),
        name="swa",
    )(sinks.astype(F32), pos_col, pos_col, proj, proj, proj,
      jnp.tile(q_norm, 2).reshape(1, LANES), jnp.tile(k_norm, 2).reshape(1, LANES), freq_row, blockdiag)


def _ssd_kernel(xbc_ref, z0_ref, z1_ref, dt_ref, cw_ref, cb_ref, dtb_ref, alog_ref, dsk_ref, ng_ref, e_ref, bm_ref,
                o_ref, buf_ref, st_ref):
    n = pl.program_id(1)
    L = SSD_CHUNK
    gw = SSD_GROUP_W
    hp = SSD_HEAD_DIM
    ns = SSD_D_STATE

    @pl.when(n == 0)
    def _():
        buf_ref[0:8, :] = jnp.zeros((8, SSD_CONV_CH), F32)
        st_ref[...] = jnp.zeros_like(st_ref)

    cur = xbc_ref[...]
    buf_ref[8:8 + L, :] = cur
    acc = cb_ref[...] + cw_ref[SSD_CONV - 1:SSD_CONV, :] * cur
    for j in range(SSD_CONV - 1):
        acc = acc + cw_ref[j:j + 1, :] * buf_ref[pl.ds(8 - (SSD_CONV - 1) + j, L), :]
    buf_ref[0:8, :] = cur[L - 8:L, :]
    xbc = _silu(acc)

    lane = lax.broadcasted_iota(jnp.int32, (1, LANES), 1)
    dt = _softplus(dt_ref[...] + dtb_ref[...])
    a_neg = jnp.where(lane < SSD_HEADS, -jnp.exp(alog_ref[...]), 0.0)
    ri = lax.broadcasted_iota(jnp.int32, (L, L), 0)
    ci = lax.broadcasted_iota(jnp.int32, (L, L), 1)
    tril = jnp.where(ri >= ci, 1.0, 0.0).astype(BF16)
    acum = _select_dot_lhs(tril, dt * a_neg, 3)
    acum_parts = _split_bf16(acum, 3)
    dt_parts = _split_bf16(dt, 2)
    acum_t = jnp.concatenate([acum, acum], axis=0).T

    lane_lo = lane < hp
    row_l = lax.broadcasted_iota(jnp.int32, (L, gw), 0)
    src_s = lax.broadcasted_iota(jnp.int32, (L, gw), 1) & (hp - 1)
    causal = src_s <= row_l
    blockmask = bm_ref[...]

    heads_per_group = SSD_HEADS // SSD_GROUPS
    for g in range(SSD_GROUPS):
        cols = pl.ds(g * gw, gw)
        e_g = e_ref[:, cols]
        col = sum(_dot(p, e_g) for p in acum_parts)
        dt_exp = sum(_dot(p, e_g) for p in dt_parts)
        row_parts = []
        for pr in range(heads_per_group // 2):
            h0 = g * heads_per_group + 2 * pr
            r = jnp.where(lane_lo, acum_t[h0:h0 + 1, :], acum_t[h0 + 1:h0 + 2, :])
            row_parts.append(jnp.broadcast_to(r, (L, LANES)))
        row = jnp.concatenate(row_parts, axis=1)
        decay = jnp.where(causal, jnp.exp(col - row), 0.0)

        xs = xbc[:, g * gw:(g + 1) * gw]
        b_g = xbc[:, SSD_D_INNER + g * ns:SSD_D_INNER + (g + 1) * ns]
        c_g = xbc[:, SSD_D_INNER + SSD_BC_W + g * ns:SSD_D_INNER + SSD_BC_W + (g + 1) * ns]
        b_bf, c_bf = b_g.astype(BF16), c_g.astype(BF16)
        cb = _dot_nt(c_bf, b_bf)
        cb2 = jnp.concatenate([cb, cb], axis=1)
        cb_exp = jnp.concatenate([cb2] * (gw // LANES), axis=1)
        m = (cb_exp * decay).astype(BF16)
        xdt = xs * dt_exp
        xdt_bf = xdt.astype(BF16)

        y_parts = []
        for qd in range(gw // 256):
            sl = slice(qd * 256, (qd + 1) * 256)
            rhs = jnp.concatenate([xdt_bf[:, sl]] * 4, axis=0) * blockmask
            y_parts.append(_dot(m[:, sl], rhs))
        y = jnp.concatenate(y_parts, axis=1)

        state = st_ref[g]
        y = y + _dot(c_bf, state.astype(BF16)) * jnp.exp(col)
        col_end = col[L - 1:L, :]
        xw = (xdt * jnp.exp(col_end - col)).astype(BF16)
        st_ref[g] = jnp.exp(col_end) * state + _dot(b_g.T.astype(BF16), xw)

        y = y + xs * dsk_ref[:, cols]
        z_ref = z0_ref if g < SSD_GROUPS // 2 else z1_ref
        zc = pl.ds((g % (SSD_GROUPS // 2)) * gw, gw)
        y = y * _silu(z_ref[:, zc])
        o_ref[:, cols] = _rms_norm(y, ng_ref[:, cols]).astype(o_ref.dtype)


def _ssd(proj, conv_w, conv_b, dt_bias, a_log, d_skip, norm_g, batch, seq):
    t = batch * seq
    L = SSD_CHUNK
    ns = seq // L
    pad = LANES - SSD_HEADS
    expand = jnp.asarray(np.kron(np.eye(LANES, SSD_HEADS), np.ones((1, SSD_HEAD_DIM))), BF16)
    blockmask = jnp.asarray(np.kron(np.eye(4), np.ones((SSD_HEAD_DIM, SSD_HEAD_DIM))), BF16)
    row = lambda b, n: b * ns + n
    const = lambda b, n: (0, 0)
    half = SSD_D_INNER // 2
    vmem = (2 * L * (SSD_CONV_CH + SSD_D_INNER + LANES) * 4 + 2 * L * SSD_D_INNER * 2
            + 2 * (6 * SSD_CONV_CH * 4 + LANES * SSD_D_INNER * 2)
            + (8 + L) * SSD_CONV_CH * 4 + SSD_GROUPS * SSD_D_STATE * SSD_GROUP_W * 4 + (12 << 20))
    return pl.pallas_call(
        _ssd_kernel,
        grid=(batch, ns),
        in_specs=[
            pl.BlockSpec((L, SSD_CONV_CH), lambda b, n: (row(b, n), SSD_XBC // SSD_CONV_CH)),
            pl.BlockSpec((L, half), lambda b, n: (row(b, n), SSD_Z // half)),
            pl.BlockSpec((L, half), lambda b, n: (row(b, n), SSD_Z // half + 1)),
            pl.BlockSpec((L, LANES), lambda b, n: (row(b, n), SSD_DT // LANES)),
            pl.BlockSpec((SSD_CONV, SSD_CONV_CH), const),
            pl.BlockSpec((1, SSD_CONV_CH), const),
            pl.BlockSpec((1, LANES), const),
            pl.BlockSpec((1, LANES), const),
            pl.BlockSpec((1, SSD_D_INNER), const),
            pl.BlockSpec((1, SSD_D_INNER), const),
            pl.BlockSpec((LANES, SSD_D_INNER), const),
            pl.BlockSpec((256, 256), const),
        ],
        out_specs=pl.BlockSpec((L, SSD_D_INNER), lambda b, n: (row(b, n), 0)),
        out_shape=jax.ShapeDtypeStruct((t, SSD_D_INNER), BF16),
        scratch_shapes=[pltpu.VMEM((8 + L, SSD_CONV_CH), F32),
                        pltpu.VMEM((SSD_GROUPS, SSD_D_STATE, SSD_GROUP_W), F32)],
        compiler_params=_compiler_params(("parallel", "arbitrary"), vmem),
        name="ssd",
    )(proj, proj, proj, proj, conv_w, conv_b.reshape(1, -1),
      jnp.pad(dt_bias, (0, pad)).reshape(1, LANES), jnp.pad(a_log, (0, pad)).reshape(1, LANES),
      jnp.repeat(d_skip, SSD_HEAD_DIM).reshape(1, -1), norm_g.reshape(1, -1), expand, blockmask)


def _hyb_weight(w_in):
    gla = w_in[:, :3072]
    glr = w_in[:, 3072:3072 + GLA_GATE_RANK]
    swa = w_in[:, 3072 + GLA_GATE_RANK:]
    zeros = jnp.zeros((w_in.shape[0], HYB_W - HYB_GLR - GLA_GATE_RANK), w_in.dtype)
    return jnp.concatenate([gla, swa, glr, zeros], axis=1).astype(BF16)


def _ssd_weight(w_in):
    z = w_in[:, :SSD_D_INNER]
    xbc = w_in[:, SSD_D_INNER:SSD_D_INNER + SSD_CONV_CH]
    dt = w_in[:, SSD_D_INNER + SSD_CONV_CH:]
    zeros = jnp.zeros((w_in.shape[0], SSD_W - SSD_DT - SSD_HEADS), w_in.dtype)
    return jnp.concatenate([xbc, z, dt, zeros], axis=1).astype(BF16)


def _gla_swa_mixer(h, pos_col, norm_gain, w_in, gate_w2, gate_b, gla_norm, q_norm, k_norm, sinks, w_out, batch, seq):
    proj = _norm_matmul(h, norm_gain, _hyb_weight(w_in), tn=512)
    w2p = jnp.pad(gate_w2, ((0, LANES - GLA_GATE_RANK), (0, 0))).astype(BF16)
    o_gla = _gla(proj, w2p, gate_b, gla_norm, batch, seq)
    o_swa = _swa(proj, pos_col, sinks, q_norm, k_norm, batch, seq)
    w_out = w_out.astype(BF16)
    split = GLA_HEADS * GLA_DV
    return _matmul_residual([o_gla, o_swa], [w_out[:split], w_out[split:]], h)


def _ssd_mixer(h, norm_gain, w_in, conv_w, conv_b, dt_bias, a_log, d_skip, norm_g, w_out, batch, seq):
    proj = _norm_matmul(h, norm_gain, _ssd_weight(w_in), tn=1536)
    y = _ssd(proj, conv_w, conv_b, dt_bias, a_log, d_skip, norm_g, batch, seq)
    return _matmul_residual([y], [w_out.astype(BF16)], h)


def kernel(x, positions, norm_ffn, w_ffn_gu, w_ffn_down, norm_mix, hyb_w_in, gla_gate_w2, gla_gate_b, gla_norm,
           attn_q_norm, attn_k_norm, attn_sinks, hyb_w_out, ssd_w_in, ssd_conv_w, ssd_conv_b, ssd_dt_bias, ssd_a_log,
           ssd_d, ssd_norm, ssd_w_out):
    batch, seq, d = x.shape
    h = x.reshape(batch * seq, d)
    pos_col = positions.reshape(batch * seq, 1)
    depth = norm_ffn.shape[0]
    for layer in range(depth):
        i = layer // 2
        h = _ffn(h, norm_ffn[layer, 0], w_ffn_gu[layer, 0].astype(BF16), w_ffn_down[layer, 0].astype(BF16))
        if layer % 2 == 0:
            h = _gla_swa_mixer(h, pos_col, norm_mix[layer], hyb_w_in[i], gla_gate_w2[i], gla_gate_b[i], gla_norm[i],
                               attn_q_norm[i], attn_k_norm[i], attn_sinks[i], hyb_w_out[i], batch, seq)
        else:
            h = _ssd_mixer(h, norm_mix[layer], ssd_w_in[i], ssd_conv_w[i], ssd_conv_b[i], ssd_dt_bias[i],
                           ssd_a_log[i], ssd_d[i], ssd_norm[i], ssd_w_out[i], batch, seq)
        h = _ffn(h, norm_ffn[layer, 1], w_ffn_gu[layer, 1].astype(BF16), w_ffn_down[layer, 1].astype(BF16))
    return h.reshape(batch, seq, d)
```

```python
import functools

import numpy as np
import jax
import jax.numpy as jnp
from jax import lax
from jax.experimental import pallas as pl
from jax.experimental.pallas import tpu as pltpu

F32 = jnp.float32
BF16 = jnp.bfloat16

D_MODEL = 2048
D_FF = 5632
NORM_EPS = 1e-6
MACARON_WEIGHT = 0.5

GLA_HEADS = 4
GLA_DK = 128
GLA_DV = 256
GLA_GATE_RANK = 16
GLA_TAU = 16.0
GLA_CHUNK = 64

SWA_HEAD_DIM = 64
SWA_Q_HEADS = 16
SWA_KV_HEADS = 2
SWA_WINDOW = 128
SWA_BLOCK = 128
ROPE_THETA = 500000.0
ROPE_DIM = 16

SSD_D_INNER = 4096
SSD_HEAD_DIM = 64
SSD_HEADS = 64
SSD_GROUPS = 8
SSD_D_STATE = 128
SSD_CONV = 4
SSD_CHUNK = 64
SSD_GROUP_W = SSD_D_INNER // SSD_GROUPS
SSD_BC_W = SSD_GROUPS * SSD_D_STATE
SSD_CONV_CH = SSD_D_INNER + 2 * SSD_BC_W

LANES = 128
VMEM_LIMIT_CAP = 56 * 1024 * 1024

HYB_Q, HYB_K, HYB_V, HYB_R, HYB_SQ, HYB_SK, HYB_SV, HYB_GLR = 0, 512, 1024, 2048, 3072, 4096, 4224, 4352
HYB_W = 4608
SSD_Z, SSD_X, SSD_B, SSD_C, SSD_DT = 0, 4096, 8192, 9216, 10240
SSD_W = 10752


def _compiler_params(semantics, vmem_bytes):
    limit = min(int(vmem_bytes * 1.25) + (4 << 20), VMEM_LIMIT_CAP)
    return pltpu.CompilerParams(dimension_semantics=semantics, vmem_limit_bytes=limit)


def _rms_norm(x, gain):
    ms = jnp.mean(x * x, axis=-1, keepdims=True)
    return x * lax.rsqrt(ms + NORM_EPS) * gain


def _silu(x):
    return x * jax.nn.sigmoid(x)


def _softplus(x):
    return jnp.maximum(x, 0.0) + jnp.log1p(jnp.exp(-jnp.abs(x)))


def _split_bf16(x, n):
    parts, rest = [], x
    for _ in range(n):
        p = rest.astype(BF16)
        parts.append(p)
        rest = rest - p.astype(F32)
    return parts


def _dot(a, b):
    return jnp.dot(a, b, preferred_element_type=F32)


def _dot_nt(a, b):
    return lax.dot_general(a, b, (((1,), (1,)), ((), ())), preferred_element_type=F32)


def _select_dot_lhs(sel, x, n):
    return sum(_dot(sel, p) for p in _split_bf16(x, n))


def _select_dot_rhs(x, sel, n):
    return sum(_dot(p, sel) for p in _split_bf16(x, n))


def _ffn_kernel(x_ref, g_ref, wg_ref, wu_ref, wd_ref, o_ref, hn_ref):
    @pl.when(pl.program_id(1) == 0)
    def _():
        x = x_ref[...]
        hn_ref[...] = _rms_norm(x, g_ref[...]).astype(BF16)
        o_ref[...] = x

    h = hn_ref[...]
    gate = _dot(h, wg_ref[...])
    up = _dot(h, wu_ref[...])
    act = (_silu(gate) * (up * MACARON_WEIGHT)).astype(BF16)
    o_ref[...] += _dot(act, wd_ref[...])


def _ffn(h, gain, w_gu, w_down, tm=1024, tf=512):
    t, d = h.shape
    nf = D_FF // tf
    vmem = 2 * (2 * tm * d * 4) + 2 * 3 * d * tf * 2 + tm * d * 2 + 3 * tm * tf * 4
    return pl.pallas_call(
        _ffn_kernel,
        grid=(t // tm, nf),
        in_specs=[
            pl.BlockSpec((tm, d), lambda i, j: (i, 0)),
            pl.BlockSpec((1, d), lambda i, j: (0, 0)),
            pl.BlockSpec((d, tf), lambda i, j: (0, j)),
            pl.BlockSpec((d, tf), lambda i, j: (0, j + nf)),
            pl.BlockSpec((tf, d), lambda i, j: (j, 0)),
        ],
        out_specs=pl.BlockSpec((tm, d), lambda i, j: (i, 0)),
        out_shape=jax.ShapeDtypeStruct((t, d), F32),
        scratch_shapes=[pltpu.VMEM((tm, d), BF16)],
        compiler_params=_compiler_params(("parallel", "arbitrary"), vmem),
        name="ffn",
    )(h, gain.reshape(1, d), w_gu, w_gu, w_down)


def _norm_matmul_kernel(x_ref, g_ref, w_ref, o_ref, hn_ref):
    @pl.when(pl.program_id(1) == 0)
    def _():
        hn_ref[...] = _rms_norm(x_ref[...], g_ref[...]).astype(BF16)

    o_ref[...] = _dot(hn_ref[...], w_ref[...]).astype(o_ref.dtype)


def _norm_matmul(h, gain, w, tn, tm=1024):
    t, d = h.shape
    n = w.shape[1]
    vmem = 2 * tm * d * 4 + 2 * d * tn * 2 + 2 * tm * tn * 2 + tm * d * 2 + tm * tn * 4
    return pl.pallas_call(
        _norm_matmul_kernel,
        grid=(t // tm, n // tn),
        in_specs=[
            pl.BlockSpec((tm, d), lambda i, j: (i, 0)),
            pl.BlockSpec((1, d), lambda i, j: (0, 0)),
            pl.BlockSpec((d, tn), lambda i, j: (0, j)),
        ],
        out_specs=pl.BlockSpec((tm, tn), lambda i, j: (i, j)),
        out_shape=jax.ShapeDtypeStruct((t, n), BF16),
        scratch_shapes=[pltpu.VMEM((tm, d), BF16)],
        compiler_params=_compiler_params(("parallel", "arbitrary"), vmem),
        name="norm_matmul",
    )(h, gain.reshape(1, d), w)


def _matmul_residual_kernel(*refs, n_in):
    a_refs, w_refs = refs[:n_in], refs[n_in:2 * n_in]
    r_ref, o_ref = refs[2 * n_in], refs[2 * n_in + 1]
    acc = r_ref[...]
    for a_ref, w_ref in zip(a_refs, w_refs):
        acc = acc + _dot(a_ref[...], w_ref[...])
    o_ref[...] = acc


def _matmul_residual(acts, w, res, tm=512):
    t, d = res.shape
    n_in = len(acts)
    k = acts[0].shape[1]
    assert all(a.shape[1] == k for a in acts) and w.shape == (n_in * k, d)
    vmem = 2 * 2 * tm * d * 4 + n_in * (2 * tm * k * 2 + 2 * k * d * 2)
    in_specs = [pl.BlockSpec((tm, k), lambda i: (i, 0)) for _ in acts]
    in_specs += [pl.BlockSpec((k, d), functools.partial(lambda i, idx: (idx, 0), idx=idx)) for idx in range(n_in)]
    weights = [w] * n_in
    in_specs.append(pl.BlockSpec((tm, d), lambda i: (i, 0)))
    return pl.pallas_call(
        functools.partial(_matmul_residual_kernel, n_in=n_in),
        grid=(t // tm,),
        in_specs=in_specs,
        out_specs=pl.BlockSpec((tm, d), lambda i: (i, 0)),
        out_shape=jax.ShapeDtypeStruct((t, d), F32),
        compiler_params=_compiler_params(("parallel",), vmem),
        name="matmul_residual",
    )(*acts, *weights, res)


def _gla_kernel(q_ref, k_ref, v_ref, r_ref, glr_ref, w2_ref, gb_ref, gn_ref, o_ref, st_ref, *, n_chunks):
    @pl.when(pl.program_id(2) == 0)
    def _():
        st_ref[...] = jnp.zeros_like(st_ref)

    c = GLA_CHUNK
    ri = lax.broadcasted_iota(jnp.int32, (c, c), 0)
    ci = lax.broadcasted_iota(jnp.int32, (c, c), 1)
    causal = ri >= ci
    tril = jnp.where(causal, 1.0, 0.0).astype(BF16)
    w2, gb, gn = w2_ref[...], gb_ref[...], gn_ref[...]
    state = st_ref[...]
    for idx in range(n_chunks):
        rows = pl.ds(idx * c, c)
        z = _dot(glr_ref[rows, :], w2) + gb
        log_a = (jnp.minimum(z, 0.0) - jnp.log1p(jnp.exp(-jnp.abs(z)))) * (1.0 / GLA_TAU)
        bcum = _select_dot_lhs(tril, log_a, 3)
        b_last = bcum[c - 1:c, :]
        q = q_ref[rows, :].astype(F32) * (GLA_DK ** -0.5)
        k = k_ref[rows, :].astype(F32)
        v = v_ref[rows, :]
        q_in = (q * jnp.exp(bcum)).astype(BF16)
        k_in = (k * jnp.exp(-bcum)).astype(BF16)
        k_end = (k * jnp.exp(b_last - bcum)).astype(BF16)
        attn = jnp.where(causal, _dot_nt(q_in, k_in), 0.0).astype(BF16)
        o = _dot(attn, v) + _dot_nt(q_in, state.astype(BF16))
        state = jnp.exp(b_last) * state + _dot(v.astype(F32).T.astype(BF16), k_end)
        o = _rms_norm(o, gn) * _silu(r_ref[rows, :].astype(F32))
        o_ref[rows, :] = o.astype(o_ref.dtype)
    st_ref[...] = state


def _gla(proj, w2p, gate_b, gla_norm, batch, seq, lb=256):
    t = batch * seq
    ns = seq // lb
    row = lambda b, h, n: b * ns + n
    vmem = 2 * lb * (2 * GLA_DK + 2 * GLA_DV + LANES) * 4 + 2 * lb * GLA_DV * 2 + (1 << 20)
    return pl.pallas_call(
        functools.partial(_gla_kernel, n_chunks=lb // GLA_CHUNK),
        grid=(batch, GLA_HEADS, ns),
        in_specs=[
            pl.BlockSpec((lb, GLA_DK), lambda b, h, n: (row(b, h, n), HYB_Q // GLA_DK + h)),
            pl.BlockSpec((lb, GLA_DK), lambda b, h, n: (row(b, h, n), HYB_K // GLA_DK + h)),
            pl.BlockSpec((lb, GLA_DV), lambda b, h, n: (row(b, h, n), HYB_V // GLA_DV + h)),
            pl.BlockSpec((lb, GLA_DV), lambda b, h, n: (row(b, h, n), HYB_R // GLA_DV + h)),
            pl.BlockSpec((lb, LANES), lambda b, h, n: (row(b, h, n), HYB_GLR // LANES)),
            pl.BlockSpec((LANES, GLA_DK), lambda b, h, n: (0, h)),
            pl.BlockSpec((1, GLA_DK), lambda b, h, n: (0, h)),
            pl.BlockSpec((1, GLA_DV), lambda b, h, n: (0, 0)),
        ],
        out_specs=pl.BlockSpec((lb, GLA_DV), lambda b, h, n: (row(b, h, n), h)),
        out_shape=jax.ShapeDtypeStruct((t, GLA_HEADS * GLA_DV), BF16),
        scratch_shapes=[pltpu.VMEM((GLA_DV, GLA_DK), F32)],
        compiler_params=_compiler_params(("parallel", "parallel", "arbitrary"), vmem),
        name="gla",
    )(proj, proj, proj, proj, proj, w2p, gate_b.reshape(1, -1), gla_norm.reshape(1, -1))


def _swa_kernel(sink_ref, posq_ref, posp_ref, q_ref, kvc_ref, kvp_ref, qn_ref, kn_ref, freq_ref, bd_ref, o_ref):
    n = pl.program_id(1)
    w = SWA_BLOCK
    lane = lax.broadcasted_iota(jnp.int32, (w, LANES), 1)
    lane_lo = lane < SWA_HEAD_DIM
    lane_lo_kv = lax.broadcasted_iota(jnp.int32, (2 * w, LANES), 1) < SWA_HEAD_DIM
    dim = lane & (SWA_HEAD_DIM - 1)
    half = ROPE_DIM // 2
    sin_hi = (dim >= half) & (dim < ROPE_DIM)
    sin_lo = dim < half
    freq = freq_ref[...]
    bd = bd_ref[...]

    def rope_tables(pos_ref):
        ang = pos_ref[...].astype(F32) * freq
        cos, sin = jnp.cos(ang), jnp.sin(ang)
        return cos, jnp.where(sin_hi, sin, 0.0), jnp.where(sin_lo, -sin, 0.0)

    def norm_rope(x, gain, tabs):
        ms = _select_dot_rhs(x * x, bd, 2) * (1.0 / SWA_HEAD_DIM)
        xn = x * lax.rsqrt(ms + NORM_EPS) * gain
        cos, s_hi, s_lo = tabs
        return xn * cos + pltpu.roll(xn, half, 1) * s_hi + pltpu.roll(xn, LANES - half, 1) * s_lo

    tq = rope_tables(posq_ref)
    tp = rope_tables(posp_ref)
    kn = kn_ref[...]
    kk = jnp.concatenate([norm_rope(kvp_ref[:, 0:LANES].astype(F32), kn, tp),
                          norm_rope(kvc_ref[:, 0:LANES].astype(F32), kn, tq)], axis=0)
    vv = jnp.concatenate([kvp_ref[:, LANES:2 * LANES], kvc_ref[:, LANES:2 * LANES]], axis=0).astype(F32)
    kk_sw = pltpu.roll(kk, SWA_HEAD_DIM, 1)
    vv_sw = pltpu.roll(vv, SWA_HEAD_DIM, 1)
    k2 = [jnp.where(lane_lo_kv, kk, kk_sw).astype(BF16), jnp.where(lane_lo_kv, kk_sw, kk).astype(BF16)]
    v2 = [jnp.where(lane_lo_kv, vv, vv_sw).astype(BF16), jnp.where(lane_lo_kv, vv_sw, vv).astype(BF16)]

    qi = lax.broadcasted_iota(jnp.int32, (w, 2 * w), 0)
    ki = lax.broadcasted_iota(jnp.int32, (w, 2 * w), 1)
    rel = qi + w - ki
    valid = (rel >= 0) & (rel < SWA_WINDOW) & ((ki >= w) | (n > 0))

    qn = qn_ref[...]
    heads_per_kv = SWA_Q_HEADS // SWA_KV_HEADS
    for pair in range(SWA_Q_HEADS // 2):
        g = (2 * pair) // heads_per_kv
        cols = pl.ds(pair * LANES, LANES)
        qp = norm_rope(q_ref[:, cols].astype(F32), qn, tq)
        outs = []
        for hf in range(2):
            keep = lane_lo if hf == 0 else jnp.logical_not(lane_lo)
            qm = jnp.where(keep, qp, 0.0).astype(BF16)
            s = _dot_nt(qm, k2[g]) * (SWA_HEAD_DIM ** -0.5)
            s = jnp.where(valid, s, -1e30)
            sink = sink_ref[2 * pair + hf]
            m = jnp.maximum(jnp.max(s, axis=-1, keepdims=True), sink)
            p = jnp.exp(s - m)
            denom = jnp.sum(p, axis=-1, keepdims=True) + jnp.exp(sink - m)
            outs.append(_dot(p.astype(BF16), v2[g]) * (1.0 / denom))
        o_ref[:, cols] = jnp.where(lane_lo, outs[0], outs[1]).astype(o_ref.dtype)


def _swa(proj, pos_col, sinks, q_norm, k_norm, batch, seq):
    t = batch * seq
    w = SWA_BLOCK
    nb = seq // w
    half = ROPE_DIM // 2
    inv_freq = ROPE_THETA ** (-2.0 * jnp.arange(half, dtype=F32) / ROPE_DIM)
    dim = np.arange(LANES) % SWA_HEAD_DIM
    freq_row = jnp.where(dim < ROPE_DIM, inv_freq[dim % half], 0.0).reshape(1, LANES).astype(F32)
    blockdiag = jnp.asarray(np.kron(np.eye(LANES // SWA_HEAD_DIM), np.ones((SWA_HEAD_DIM, SWA_HEAD_DIM))), BF16)
    cur = lambda b, n: b * nb + n
    prev = lambda b, n: b * nb + jnp.maximum(n - 1, 0)
    q_w = SWA_Q_HEADS * SWA_HEAD_DIM
    vmem = 2 * w * (q_w + 2 * 2 * LANES) * 4 + 2 * w * q_w * 2 + 4 * w * LANES * 4 + (8 << 20)
    return pl.pallas_call(
        _swa_kernel,
        grid=(batch, nb),
        in_specs=[
            pl.BlockSpec(memory_space=pltpu.SMEM),
            pl.BlockSpec((w, 1), lambda b, n: (cur(b, n), 0)),
            pl.BlockSpec((w, 1), lambda b, n: (prev(b, n), 0)),
            pl.BlockSpec((w, q_w), lambda b, n: (cur(b, n), HYB_SQ // q_w)),
            pl.BlockSpec((w, 2 * LANES), lambda b, n: (cur(b, n), HYB_SK // (2 * LANES))),
            pl.BlockSpec((w, 2 * LANES), lambda b, n: (prev(b, n), HYB_SK // (2 * LANES))),
            pl.BlockSpec((1, LANES), lambda b, n: (0, 0)),
            pl.BlockSpec((1, LANES), lambda b, n: (0, 0)),
            pl.BlockSpec((1, LANES), lambda b, n: (0, 0)),
            pl.BlockSpec((LANES, LANES), lambda b, n: (0, 0)),
        ],
        out_specs=pl.BlockSpec((w, q_w), lambda b, n: (cur(b, n), 0)),
        out_shape=jax.ShapeDtypeStruct((t, q_w), BF16),
        compiler_params=_compiler_params(("parallel", "arbitrary"), vmem),
        name="swa",
    )(sinks.astype(F32), pos_col, pos_col, proj, proj, proj,
      jnp.tile(q_norm, 2).reshape(1, LANES), jnp.tile(k_norm, 2).reshape(1, LANES), freq_row, blockdiag)


def _ssd_kernel(x_ref, b_ref, c_ref, z_ref, dt_ref, cw_ref, cb_ref, dtb_ref, alog_ref, dsk_ref, ng_ref, e_ref, bm_ref,
                o_ref, buf_ref, st_ref):
    n = pl.program_id(1)
    L = SSD_CHUNK
    gw = SSD_GROUP_W
    hp = SSD_HEAD_DIM
    ns = SSD_D_STATE

    @pl.when(n == 0)
    def _():
        buf_ref[0:8, :] = jnp.zeros((8, SSD_CONV_CH), F32)
        st_ref[...] = jnp.zeros_like(st_ref)

    cur = jnp.concatenate([x_ref[...], b_ref[...], c_ref[...]], axis=1).astype(F32)
    buf_ref[8:8 + L, :] = cur
    acc = cb_ref[...] + cw_ref[SSD_CONV - 1:SSD_CONV, :] * cur
    for j in range(SSD_CONV - 1):
        acc = acc + cw_ref[j:j + 1, :] * buf_ref[pl.ds(8 - (SSD_CONV - 1) + j, L), :]
    buf_ref[0:8, :] = cur[L - 8:L, :]
    xbc = _silu(acc)

    lane = lax.broadcasted_iota(jnp.int32, (1, LANES), 1)
    dt = _softplus(dt_ref[...].astype(F32) + dtb_ref[...])
    a_neg = jnp.where(lane < SSD_HEADS, -jnp.exp(alog_ref[...]), 0.0)
    ri = lax.broadcasted_iota(jnp.int32, (L, L), 0)
    ci = lax.broadcasted_iota(jnp.int32, (L, L), 1)
    tril = jnp.where(ri >= ci, 1.0, 0.0).astype(BF16)
    acum = _select_dot_lhs(tril, dt * a_neg, 3)
    acum_parts = _split_bf16(acum, 3)
    dt_parts = _split_bf16(dt, 2)
    acum_t = jnp.concatenate([acum, acum], axis=0).T

    lane_lo = lane < hp
    row_l = lax.broadcasted_iota(jnp.int32, (L, gw), 0)
    src_s = lax.broadcasted_iota(jnp.int32, (L, gw), 1) & (hp - 1)
    causal = src_s <= row_l
    blockmask = bm_ref[...]

    heads_per_group = SSD_HEADS // SSD_GROUPS
    for g in range(SSD_GROUPS):
        cols = pl.ds(g * gw, gw)
        e_g = e_ref[:, cols]
        col = sum(_dot(p, e_g) for p in acum_parts)
        dt_exp = sum(_dot(p, e_g) for p in dt_parts)
        row_parts = []
        for pr in range(heads_per_group // 2):
            h0 = g * heads_per_group + 2 * pr
            r = jnp.where(lane_lo, acum_t[h0:h0 + 1, :], acum_t[h0 + 1:h0 + 2, :])
            row_parts.append(jnp.broadcast_to(r, (L, LANES)))
        row = jnp.concatenate(row_parts, axis=1)
        decay = jnp.where(causal, jnp.exp(col - row), 0.0)

        xs = xbc[:, g * gw:(g + 1) * gw]
        b_g = xbc[:, SSD_D_INNER + g * ns:SSD_D_INNER + (g + 1) * ns]
        c_g = xbc[:, SSD_D_INNER + SSD_BC_W + g * ns:SSD_D_INNER + SSD_BC_W + (g + 1) * ns]
        b_bf, c_bf = b_g.astype(BF16), c_g.astype(BF16)
        cb = _dot_nt(c_bf, b_bf)
        cb2 = jnp.concatenate([cb, cb], axis=1)
        cb_exp = jnp.concatenate([cb2] * (gw // LANES), axis=1)
        m = (cb_exp * decay).astype(BF16)
        xdt = xs * dt_exp
        xdt_bf = xdt.astype(BF16)

        y_parts = []
        for qd in range(gw // 256):
            sl = slice(qd * 256, (qd + 1) * 256)
            rhs = jnp.concatenate([xdt_bf[:, sl]] * 4, axis=0) * blockmask
            y_parts.append(_dot(m[:, sl], rhs))
        y = jnp.concatenate(y_parts, axis=1)

        state = st_ref[g]
        y = y + _dot(c_bf, state.astype(BF16)) * jnp.exp(col)
        col_end = col[L - 1:L, :]
        xw = (xdt * jnp.exp(col_end - col)).astype(BF16)
        st_ref[g] = jnp.exp(col_end) * state + _dot(b_g.T.astype(BF16), xw)

        y = y + xs * dsk_ref[:, cols]
        y = y * _silu(z_ref[:, cols].astype(F32))
        o_ref[:, cols] = _rms_norm(y, ng_ref[:, cols]).astype(o_ref.dtype)


def _ssd(proj, conv_w, conv_b, dt_bias, a_log, d_skip, norm_g, batch, seq):
    t = batch * seq
    L = SSD_CHUNK
    ns = seq // L
    pad = LANES - SSD_HEADS
    expand = jnp.asarray(np.kron(np.eye(LANES, SSD_HEADS), np.ones((1, SSD_HEAD_DIM))), BF16)
    blockmask = jnp.asarray(np.kron(np.eye(4), np.ones((SSD_HEAD_DIM, SSD_HEAD_DIM))), BF16)
    row = lambda b, n: b * ns + n
    const = lambda b, n: (0, 0)
    vmem = (2 * L * (SSD_CONV_CH + SSD_D_INNER + LANES) * 2 + 2 * L * SSD_D_INNER * 2
            + 2 * (6 * SSD_CONV_CH * 4 + LANES * SSD_D_INNER * 2)
            + (8 + L) * SSD_CONV_CH * 4 + SSD_GROUPS * SSD_D_STATE * SSD_GROUP_W * 4 + (12 << 20))
    return pl.pallas_call(
        _ssd_kernel,
        grid=(batch, ns),
        in_specs=[
            pl.BlockSpec((L, SSD_D_INNER), lambda b, n: (row(b, n), SSD_X // SSD_D_INNER)),
            pl.BlockSpec((L, SSD_BC_W), lambda b, n: (row(b, n), SSD_B // SSD_BC_W)),
            pl.BlockSpec((L, SSD_BC_W), lambda b, n: (row(b, n), SSD_C // SSD_BC_W)),
            pl.BlockSpec((L, SSD_D_INNER), lambda b, n: (row(b, n), SSD_Z // SSD_D_INNER)),
            pl.BlockSpec((L, LANES), lambda b, n: (row(b, n), SSD_DT // LANES)),
            pl.BlockSpec((SSD_CONV, SSD_CONV_CH), const),
            pl.BlockSpec((1, SSD_CONV_CH), const),
            pl.BlockSpec((1, LANES), const),
            pl.BlockSpec((1, LANES), const),
            pl.BlockSpec((1, SSD_D_INNER), const),
            pl.BlockSpec((1, SSD_D_INNER), const),
            pl.BlockSpec((LANES, SSD_D_INNER), const),
            pl.BlockSpec((256, 256), const),
        ],
        out_specs=pl.BlockSpec((L, SSD_D_INNER), lambda b, n: (row(b, n), 0)),
        out_shape=jax.ShapeDtypeStruct((t, SSD_D_INNER), BF16),
        scratch_shapes=[pltpu.VMEM((8 + L, SSD_CONV_CH), F32),
                        pltpu.VMEM((SSD_GROUPS, SSD_D_STATE, SSD_GROUP_W), F32)],
        compiler_params=_compiler_params(("parallel", "arbitrary"), vmem),
        name="ssd",
    )(proj, proj, proj, proj, proj, conv_w, conv_b.reshape(1, -1),
      jnp.pad(dt_bias, (0, pad)).reshape(1, LANES), jnp.pad(a_log, (0, pad)).reshape(1, LANES),
      jnp.repeat(d_skip, SSD_HEAD_DIM).reshape(1, -1), norm_g.reshape(1, -1), expand, blockmask)


def _hyb_weight(w_in):
    w = w_in.astype(BF16)
    glr_end = HYB_SQ + GLA_GATE_RANK
    zeros = jnp.zeros((w.shape[0], HYB_W - HYB_GLR - GLA_GATE_RANK), BF16)
    return jnp.concatenate([w[:, :HYB_SQ], w[:, glr_end:], w[:, HYB_SQ:glr_end], zeros], axis=1)


def _ssd_weight(w_in):
    return jnp.pad(w_in.astype(BF16), ((0, 0), (0, SSD_W - w_in.shape[1])))


def _gla_swa_mixer(h, pos_col, norm_gain, w_in, gate_w2, gate_b, gla_norm, q_norm, k_norm, sinks, w_out, batch, seq):
    proj = _norm_matmul(h, norm_gain, _hyb_weight(w_in), tn=1536)
    w2p = jnp.pad(gate_w2, ((0, LANES - GLA_GATE_RANK), (0, 0))).astype(BF16)
    o_gla = _gla(proj, w2p, gate_b, gla_norm, batch, seq)
    o_swa = _swa(proj, pos_col, sinks, q_norm, k_norm, batch, seq)
    return _matmul_residual([o_gla, o_swa], w_out.astype(BF16), h)


def _ssd_mixer(h, norm_gain, w_in, conv_w, conv_b, dt_bias, a_log, d_skip, norm_g, w_out, batch, seq):
    proj = _norm_matmul(h, norm_gain, _ssd_weight(w_in), tn=1536)
    y = _ssd(proj, conv_w, conv_b, dt_bias, a_log, d_skip, norm_g, batch, seq)
    return _matmul_residual([y], w_out.astype(BF16), h)


def kernel(x, positions, norm_ffn, w_ffn_gu, w_ffn_down, norm_mix, hyb_w_in, gla_gate_w2, gla_gate_b, gla_norm,
           attn_q_norm, attn_k_norm, attn_sinks, hyb_w_out, ssd_w_in, ssd_conv_w, ssd_conv_b, ssd_dt_bias, ssd_a_log,
           ssd_d, ssd_norm, ssd_w_out):
    batch, seq, d = x.shape
    h = x.reshape(batch * seq, d)
    pos_col = positions.reshape(batch * seq, 1)
    depth = norm_ffn.shape[0]
    for layer in range(depth):
        i = layer // 2
        h = _ffn(h, norm_ffn[layer, 0], w_ffn_gu[layer, 0].astype(BF16), w_ffn_down[layer, 0].astype(BF16))
        if layer % 2 == 0:
            h = _gla_swa_mixer(h, pos_col, norm_mix[layer], hyb_w_in[i], gla_gate_w2[i], gla_gate_b[i], gla_norm[i],
                               attn_q_norm[i], attn_k_norm[i], attn_sinks[i], hyb_w_out[i], batch, seq)
        else:
            h = _ssd_mixer(h, norm_mix[layer], ssd_w_in[i], ssd_conv_w[i], ssd_conv_b[i], ssd_dt_bias[i],
                           ssd_a_log[i], ssd_d[i], ssd_norm[i], ssd_w_out[i], batch, seq)
        h = _ffn(h, norm_ffn[layer, 1], w_ffn_gu[layer, 1].astype(BF16), w_ffn_down[layer, 1].astype(BF16))
    return h.reshape(batch, seq, d)
```

```python
import functools

import numpy as np
import jax
import jax.numpy as jnp
from jax import lax
from jax.experimental import pallas as pl
from jax.experimental.pallas import tpu as pltpu

F32 = jnp.float32
BF16 = jnp.bfloat16

D_MODEL = 2048
D_FF = 5632
NORM_EPS = 1e-6
MACARON_WEIGHT = 0.5

GLA_HEADS = 4
GLA_DK = 128
GLA_DV = 256
GLA_GATE_RANK = 16
GLA_TAU = 16.0
GLA_CHUNK = 64

SWA_HEAD_DIM = 64
SWA_Q_HEADS = 16
SWA_KV_HEADS = 2
SWA_WINDOW = 128
SWA_BLOCK = 128
ROPE_THETA = 500000.0
ROPE_DIM = 16

SSD_D_INNER = 4096
SSD_HEAD_DIM = 64
SSD_HEADS = 64
SSD_GROUPS = 8
SSD_D_STATE = 128
SSD_CONV = 4
SSD_CHUNK = 64
SSD_GROUP_W = SSD_D_INNER // SSD_GROUPS
SSD_BC_W = SSD_GROUPS * SSD_D_STATE
SSD_CONV_CH = SSD_D_INNER + 2 * SSD_BC_W

LANES = 128
VMEM_LIMIT_CAP = 56 * 1024 * 1024

HYB_Q, HYB_K, HYB_V, HYB_R, HYB_SQ, HYB_SK, HYB_SV, HYB_GLR = 0, 512, 1024, 2048, 3072, 4096, 4224, 4352
HYB_W = 4608
SSD_Z, SSD_X, SSD_B, SSD_C, SSD_DT = 0, 4096, 8192, 9216, 10240
SSD_W = 10752


def _compiler_params(semantics, vmem_bytes):
    limit = min(int(vmem_bytes * 1.25) + (4 << 20), VMEM_LIMIT_CAP)
    return pltpu.CompilerParams(dimension_semantics=semantics, vmem_limit_bytes=limit)


def _rms_norm(x, gain):
    ms = jnp.mean(x * x, axis=-1, keepdims=True)
    return x * lax.rsqrt(ms + NORM_EPS) * gain


def _silu(x):
    return x * jax.nn.sigmoid(x)


def _softplus(x):
    return jnp.maximum(x, 0.0) + jnp.log1p(jnp.exp(-jnp.abs(x)))


def _split_bf16(x, n):
    parts, rest = [], x
    for _ in range(n):
        p = rest.astype(BF16)
        parts.append(p)
        rest = rest - p.astype(F32)
    return parts


def _dot(a, b):
    return jnp.dot(a, b, preferred_element_type=F32)


def _dot_nt(a, b):
    return lax.dot_general(a, b, (((1,), (1,)), ((), ())), preferred_element_type=F32)


def _select_dot_lhs(sel, x, n):
    return sum(_dot(sel, p) for p in _split_bf16(x, n))


def _select_dot_rhs(x, sel, n):
    return sum(_dot(p, sel) for p in _split_bf16(x, n))


def _ffn_kernel(x_ref, g_ref, wg_ref, wu_ref, wd_ref, o_ref, hn_ref):
    @pl.when(pl.program_id(1) == 0)
    def _():
        x = x_ref[...]
        hn_ref[...] = _rms_norm(x, g_ref[...]).astype(BF16)
        o_ref[...] = x

    h = hn_ref[...]
    gate = _dot(h, wg_ref[...])
    up = _dot(h, wu_ref[...])
    act = (_silu(gate) * (up * MACARON_WEIGHT)).astype(BF16)
    o_ref[...] += _dot(act, wd_ref[...])


def _ffn(h, gain, w_gu, w_down, layer, pos, tm=1024, tf=512):
    t, d = h.shape
    nf = D_FF // tf
    vmem = 2 * (2 * tm * d * 4) + 2 * 3 * d * tf * 2 + tm * d * 2 + 3 * tm * tf * 4
    return pl.pallas_call(
        _ffn_kernel,
        grid=(t // tm, nf),
        in_specs=[
            pl.BlockSpec((tm, d), lambda i, j: (i, 0)),
            pl.BlockSpec((1, d), lambda i, j: (0, 0)),
            pl.BlockSpec((None, None, d, tf), lambda i, j: (layer, pos, 0, j)),
            pl.BlockSpec((None, None, d, tf), lambda i, j: (layer, pos, 0, j + nf)),
            pl.BlockSpec((None, None, tf, d), lambda i, j: (layer, pos, j, 0)),
        ],
        out_specs=pl.BlockSpec((tm, d), lambda i, j: (i, 0)),
        out_shape=jax.ShapeDtypeStruct((t, d), F32),
        scratch_shapes=[pltpu.VMEM((tm, d), BF16)],
        compiler_params=_compiler_params(("parallel", "arbitrary"), vmem),
        name="ffn",
    )(h, gain.reshape(1, d), w_gu, w_gu, w_down)


def _norm_matmul_kernel(x_ref, g_ref, w_ref, o_ref, hn_ref):
    @pl.when(pl.program_id(1) == 0)
    def _():
        hn_ref[...] = _rms_norm(x_ref[...], g_ref[...]).astype(BF16)

    o_ref[...] = _dot(hn_ref[...], w_ref[...]).astype(o_ref.dtype)


def _norm_matmul(h, gain, w, tn, tm=1024):
    t, d = h.shape
    n = w.shape[1]
    vmem = 2 * tm * d * 4 + 2 * d * tn * 2 + 2 * tm * tn * 2 + tm * d * 2 + tm * tn * 4
    return pl.pallas_call(
        _norm_matmul_kernel,
        grid=(t // tm, n // tn),
        in_specs=[
            pl.BlockSpec((tm, d), lambda i, j: (i, 0)),
            pl.BlockSpec((1, d), lambda i, j: (0, 0)),
            pl.BlockSpec((d, tn), lambda i, j: (0, j)),
        ],
        out_specs=pl.BlockSpec((tm, tn), lambda i, j: (i, j)),
        out_shape=jax.ShapeDtypeStruct((t, n), BF16),
        scratch_shapes=[pltpu.VMEM((tm, d), BF16)],
        compiler_params=_compiler_params(("parallel", "arbitrary"), vmem),
        name="norm_matmul",
    )(h, gain.reshape(1, d), w)


def _matmul_residual_kernel(*refs, n_in):
    a_refs, w_refs = refs[:n_in], refs[n_in:2 * n_in]
    r_ref, o_ref = refs[2 * n_in], refs[2 * n_in + 1]
    acc = r_ref[...]
    for a_ref, w_ref in zip(a_refs, w_refs):
        acc = acc + _dot(a_ref[...], w_ref[...])
    o_ref[...] = acc


def _matmul_residual(acts, w, res, tm=512):
    t, d = res.shape
    n_in = len(acts)
    k = acts[0].shape[1]
    assert all(a.shape[1] == k for a in acts) and w.shape == (n_in * k, d)
    vmem = 2 * 2 * tm * d * 4 + n_in * (2 * tm * k * 2 + 2 * k * d * 2)
    in_specs = [pl.BlockSpec((tm, k), lambda i: (i, 0)) for _ in acts]
    in_specs += [pl.BlockSpec((k, d), functools.partial(lambda i, idx: (idx, 0), idx=idx)) for idx in range(n_in)]
    weights = [w] * n_in
    in_specs.append(pl.BlockSpec((tm, d), lambda i: (i, 0)))
    return pl.pallas_call(
        functools.partial(_matmul_residual_kernel, n_in=n_in),
        grid=(t // tm,),
        in_specs=in_specs,
        out_specs=pl.BlockSpec((tm, d), lambda i: (i, 0)),
        out_shape=jax.ShapeDtypeStruct((t, d), F32),
        compiler_params=_compiler_params(("parallel",), vmem),
        name="matmul_residual",
    )(*acts, *weights, res)


def _gla_kernel(q_ref, k_ref, v_ref, r_ref, glr_ref, w2_ref, gb_ref, gn_ref, o_ref, st_ref, *, n_chunks):
    @pl.when(pl.program_id(1) == 0)
    def _():
        st_ref[...] = jnp.zeros_like(st_ref)

    c = GLA_CHUNK
    lb = n_chunks * c
    ri = lax.broadcasted_iota(jnp.int32, (lb, lb), 0)
    ci = lax.broadcasted_iota(jnp.int32, (lb, lb), 1)
    same_chunk_tril = (ri >= ci) & ((ri // c) == (ci // c))
    tril = jnp.where(same_chunk_tril, 1.0, 0.0).astype(BF16)
    causal = lax.broadcasted_iota(jnp.int32, (c, c), 0) >= lax.broadcasted_iota(jnp.int32, (c, c), 1)
    gn = gn_ref[...]

    z = _dot(glr_ref[...], w2_ref[...]) + gb_ref[...]
    log_a = (jnp.minimum(z, 0.0) - jnp.log1p(jnp.exp(-jnp.abs(z)))) * (1.0 / GLA_TAU)
    bcum = _select_dot_lhs(tril, log_a, 3)
    q_all = q_ref[...].astype(F32) * (GLA_DK ** -0.5) * jnp.exp(bcum)
    k_all = k_ref[...].astype(F32)
    k_in_all = k_all * jnp.exp(-bcum)

    for h in range(GLA_HEADS):
        kcols = slice(h * GLA_DK, (h + 1) * GLA_DK)
        vcols = slice(h * GLA_DV, (h + 1) * GLA_DV)
        state = st_ref[h]
        for idx in range(n_chunks):
            rows = slice(idx * c, (idx + 1) * c)
            b_c = bcum[rows, kcols]
            b_last = b_c[c - 1:c, :]
            q_in = q_all[rows, kcols].astype(BF16)
            k_in = k_in_all[rows, kcols].astype(BF16)
            k_end = (k_all[rows, kcols] * jnp.exp(b_last - b_c)).astype(BF16)
            v = v_ref[rows, vcols]
            attn = jnp.where(causal, _dot_nt(q_in, k_in), 0.0).astype(BF16)
            o = _dot(attn, v) + _dot_nt(q_in, state.astype(BF16))
            state = jnp.exp(b_last) * state + _dot(v.astype(F32).T.astype(BF16), k_end)
            o = _rms_norm(o, gn) * _silu(r_ref[rows, vcols].astype(F32))
            o_ref[rows, vcols] = o.astype(o_ref.dtype)
        st_ref[h] = state


def _gla(proj, w2p, gate_b, gla_norm, batch, seq, lb=256):
    t = batch * seq
    ns = seq // lb
    qk_w, v_w = GLA_HEADS * GLA_DK, GLA_HEADS * GLA_DV
    row = lambda b, n: b * ns + n
    vmem = 2 * lb * (2 * qk_w + 2 * v_w + LANES) * 2 + 2 * lb * v_w * 2 + 8 * lb * qk_w * 4 + (4 << 20)
    return pl.pallas_call(
        functools.partial(_gla_kernel, n_chunks=lb // GLA_CHUNK),
        grid=(batch, ns),
        in_specs=[
            pl.BlockSpec((lb, qk_w), lambda b, n: (row(b, n), HYB_Q // qk_w)),
            pl.BlockSpec((lb, qk_w), lambda b, n: (row(b, n), HYB_K // qk_w)),
            pl.BlockSpec((lb, v_w), lambda b, n: (row(b, n), HYB_V // v_w)),
            pl.BlockSpec((lb, v_w), lambda b, n: (row(b, n), HYB_R // v_w)),
            pl.BlockSpec((lb, LANES), lambda b, n: (row(b, n), HYB_GLR // LANES)),
            pl.BlockSpec((LANES, qk_w), lambda b, n: (0, 0)),
            pl.BlockSpec((1, qk_w), lambda b, n: (0, 0)),
            pl.BlockSpec((1, GLA_DV), lambda b, n: (0, 0)),
        ],
        out_specs=pl.BlockSpec((lb, v_w), lambda b, n: (row(b, n), 0)),
        out_shape=jax.ShapeDtypeStruct((t, v_w), BF16),
        scratch_shapes=[pltpu.VMEM((GLA_HEADS, GLA_DV, GLA_DK), F32)],
        compiler_params=_compiler_params(("parallel", "arbitrary"), vmem),
        name="gla",
    )(proj, proj, proj, proj, proj, w2p, gate_b.reshape(1, -1), gla_norm.reshape(1, -1))


def _swa_kernel(sink_ref, pos_ref, q_ref, kv_ref, qn_ref, kn_ref, freq_ref, bd_ref, o_ref, kprev_ref, vprev_ref):
    n = pl.program_id(1)
    w = SWA_BLOCK
    lane = lax.broadcasted_iota(jnp.int32, (w, LANES), 1)
    lane_lo = lane < SWA_HEAD_DIM
    lane_lo_kv = lax.broadcasted_iota(jnp.int32, (2 * w, LANES), 1) < SWA_HEAD_DIM
    dim = lane & (SWA_HEAD_DIM - 1)
    half = ROPE_DIM // 2
    bd = bd_ref[...]

    @pl.when(n == 0)
    def _():
        kprev_ref[...] = jnp.zeros_like(kprev_ref)
        vprev_ref[...] = jnp.zeros_like(vprev_ref)

    ang = pos_ref[...].astype(F32) * freq_ref[...]
    cos, sin = jnp.cos(ang), jnp.sin(ang)
    s_hi = jnp.where((dim >= half) & (dim < ROPE_DIM), sin, 0.0)
    s_lo = jnp.where(dim < half, -sin, 0.0)

    def norm_rope(x, gain):
        ms = _dot((x * x).astype(BF16), bd) * (1.0 / SWA_HEAD_DIM)
        xn = x * lax.rsqrt(ms + NORM_EPS) * gain
        return xn * cos + pltpu.roll(xn, half, 1) * s_hi + pltpu.roll(xn, LANES - half, 1) * s_lo

    k_cur = norm_rope(kv_ref[:, 0:LANES].astype(F32), kn_ref[...])
    v_cur = kv_ref[:, LANES:2 * LANES].astype(F32)
    kk = jnp.concatenate([kprev_ref[...], k_cur], axis=0)
    vv = jnp.concatenate([vprev_ref[...], v_cur], axis=0)
    kprev_ref[...] = k_cur
    vprev_ref[...] = v_cur
    kk_sw = pltpu.roll(kk, SWA_HEAD_DIM, 1)
    vv_sw = pltpu.roll(vv, SWA_HEAD_DIM, 1)
    k2 = [jnp.where(lane_lo_kv, kk, kk_sw).astype(BF16), jnp.where(lane_lo_kv, kk_sw, kk).astype(BF16)]
    v2 = [jnp.where(lane_lo_kv, vv, vv_sw).astype(BF16), jnp.where(lane_lo_kv, vv_sw, vv).astype(BF16)]

    qi = lax.broadcasted_iota(jnp.int32, (w, 2 * w), 0)
    ki = lax.broadcasted_iota(jnp.int32, (w, 2 * w), 1)
    rel = qi + w - ki
    valid = (rel >= 0) & (rel < SWA_WINDOW) & ((ki >= w) | (n > 0))

    qn = qn_ref[...] * (SWA_HEAD_DIM ** -0.5)
    heads_per_kv = SWA_Q_HEADS // SWA_KV_HEADS
    for pair in range(SWA_Q_HEADS // 2):
        g = (2 * pair) // heads_per_kv
        cols = pl.ds(pair * LANES, LANES)
        qp = norm_rope(q_ref[:, cols].astype(F32), qn)
        outs = []
        for hf in range(2):
            keep = lane_lo if hf == 0 else jnp.logical_not(lane_lo)
            qm = jnp.where(keep, qp, 0.0).astype(BF16)
            s = jnp.where(valid, _dot_nt(qm, k2[g]), -1e30)
            sink = sink_ref[2 * pair + hf]
            m = jnp.maximum(jnp.max(s, axis=-1, keepdims=True), sink)
            p = jnp.exp(s - m)
            denom = jnp.sum(p, axis=-1, keepdims=True) + jnp.exp(sink - m)
            outs.append(_dot(p.astype(BF16), v2[g]) * (1.0 / denom))
        o_ref[:, cols] = jnp.where(lane_lo, outs[0], outs[1]).astype(o_ref.dtype)


def _swa(proj, pos_col, sinks, q_norm, k_norm, batch, seq):
    t = batch * seq
    w = SWA_BLOCK
    nb = seq // w
    half = ROPE_DIM // 2
    inv_freq = ROPE_THETA ** (-2.0 * jnp.arange(half, dtype=F32) / ROPE_DIM)
    dim = np.arange(LANES) % SWA_HEAD_DIM
    freq_row = jnp.where(dim < ROPE_DIM, inv_freq[dim % half], 0.0).reshape(1, LANES).astype(F32)
    blockdiag = jnp.asarray(np.kron(np.eye(LANES // SWA_HEAD_DIM), np.ones((SWA_HEAD_DIM, SWA_HEAD_DIM))), BF16)
    cur = lambda b, n: b * nb + n
    q_w = SWA_Q_HEADS * SWA_HEAD_DIM
    vmem = 2 * w * (q_w + 2 * LANES) * 2 + 2 * w * q_w * 2 + 4 * w * LANES * 4 + (8 << 20)
    return pl.pallas_call(
        _swa_kernel,
        grid=(batch, nb),
        in_specs=[
            pl.BlockSpec(memory_space=pltpu.SMEM),
            pl.BlockSpec((w, 1), lambda b, n: (cur(b, n), 0)),
            pl.BlockSpec((w, q_w), lambda b, n: (cur(b, n), HYB_SQ // q_w)),
            pl.BlockSpec((w, 2 * LANES), lambda b, n: (cur(b, n), HYB_SK // (2 * LANES))),
            pl.BlockSpec((1, LANES), lambda b, n: (0, 0)),
            pl.BlockSpec((1, LANES), lambda b, n: (0, 0)),
            pl.BlockSpec((1, LANES), lambda b, n: (0, 0)),
            pl.BlockSpec((LANES, LANES), lambda b, n: (0, 0)),
        ],
        out_specs=pl.BlockSpec((w, q_w), lambda b, n: (cur(b, n), 0)),
        out_shape=jax.ShapeDtypeStruct((t, q_w), BF16),
        scratch_shapes=[pltpu.VMEM((w, LANES), F32), pltpu.VMEM((w, LANES), F32)],
        compiler_params=_compiler_params(("parallel", "arbitrary"), vmem),
        name="swa",
    )(sinks.astype(F32), pos_col, proj, proj,
      jnp.tile(q_norm, 2).reshape(1, LANES), jnp.tile(k_norm, 2).reshape(1, LANES), freq_row, blockdiag)


def _ssd_kernel(x_ref, b_ref, c_ref, z_ref, dt_ref, cw_ref, cb_ref, dtb_ref, alog_ref, dsk_ref, ng_ref, e_ref, bm_ref,
                o_ref, tail_ref, st_ref):
    n = pl.program_id(1)
    L = SSD_CHUNK
    gw = SSD_GROUP_W
    hp = SSD_HEAD_DIM
    ns = SSD_D_STATE

    @pl.when(n == 0)
    def _():
        tail_ref[...] = jnp.zeros_like(tail_ref)
        st_ref[...] = jnp.zeros_like(st_ref)

    cur = jnp.concatenate([x_ref[...], b_ref[...], c_ref[...]], axis=1).astype(F32)
    ext = jnp.concatenate([tail_ref[...], cur], axis=0)
    tail_ref[...] = cur[L - 8:L, :]
    acc = cb_ref[...] + cw_ref[SSD_CONV - 1:SSD_CONV, :] * cur
    for j in range(SSD_CONV - 1):
        shifted = pltpu.roll(ext, SSD_CONV - 1 - j, 0)[8:8 + L, :]
        acc = acc + cw_ref[j:j + 1, :] * shifted
    xbc = _silu(acc)

    lane = lax.broadcasted_iota(jnp.int32, (1, LANES), 1)
    dt = _softplus(dt_ref[...].astype(F32) + dtb_ref[...])
    a_neg = jnp.where(lane < SSD_HEADS, -jnp.exp(alog_ref[...]), 0.0)
    ri = lax.broadcasted_iota(jnp.int32, (L, L), 0)
    ci = lax.broadcasted_iota(jnp.int32, (L, L), 1)
    tril = jnp.where(ri >= ci, 1.0, 0.0).astype(BF16)
    acum = _select_dot_lhs(tril, dt * a_neg, 3)
    acum_cat = jnp.concatenate(_split_bf16(acum, 3), axis=1)
    dt_cat = jnp.concatenate(_split_bf16(dt, 2), axis=1)
    acum_t = jnp.concatenate([acum, acum], axis=0).T

    lane_lo = lane < hp
    row_l = lax.broadcasted_iota(jnp.int32, (L, gw), 0)
    src_s = lax.broadcasted_iota(jnp.int32, (L, gw), 1) & (hp - 1)
    causal = src_s <= row_l
    blockmask = bm_ref[...]

    heads_per_group = SSD_HEADS // SSD_GROUPS
    for g in range(SSD_GROUPS):
        cols = pl.ds(g * gw, gw)
        e_g = e_ref[:, cols]
        col = _dot(acum_cat, e_g)
        dt_exp = _dot(dt_cat, e_g[:2 * LANES, :])
        row_parts = []
        for pr in range(heads_per_group // 2):
            h0 = g * heads_per_group + 2 * pr
            r = jnp.where(lane_lo, acum_t[h0:h0 + 1, :], acum_t[h0 + 1:h0 + 2, :])
            row_parts.append(jnp.broadcast_to(r, (L, LANES)))
        row = jnp.concatenate(row_parts, axis=1)
        decay = jnp.where(causal, jnp.exp(col - row), 0.0)

        xs = xbc[:, g * gw:(g + 1) * gw]
        b_g = xbc[:, SSD_D_INNER + g * ns:SSD_D_INNER + (g + 1) * ns]
        c_g = xbc[:, SSD_D_INNER + SSD_BC_W + g * ns:SSD_D_INNER + SSD_BC_W + (g + 1) * ns]
        b_bf, c_bf = b_g.astype(BF16), c_g.astype(BF16)
        cb = _dot_nt(c_bf, b_bf)
        cb2 = jnp.concatenate([cb, cb], axis=1)
        cb_exp = jnp.concatenate([cb2] * (gw // LANES), axis=1)
        m = (cb_exp * decay).astype(BF16)
        xdt = xs * dt_exp
        xdt_bf = xdt.astype(BF16)

        y_parts = []
        for qd in range(gw // 256):
            sl = slice(qd * 256, (qd + 1) * 256)
            rhs = jnp.concatenate([xdt_bf[:, sl]] * 4, axis=0) * blockmask
            y_parts.append(_dot(m[:, sl], rhs))
        y = jnp.concatenate(y_parts, axis=1)

        state = st_ref[g]
        y = y + _dot(c_bf, state.astype(BF16)) * jnp.exp(col)
        col_end = col[L - 1:L, :]
        xw = (xdt * jnp.exp(col_end - col)).astype(BF16)
        st_ref[g] = jnp.exp(col_end) * state + _dot(b_g.T.astype(BF16), xw)

        y = y + xs * dsk_ref[:, cols]
        y = y * _silu(z_ref[:, cols].astype(F32))
        o_ref[:, cols] = _rms_norm(y, ng_ref[:, cols]).astype(o_ref.dtype)


def _ssd(proj, conv_w, conv_b, dt_bias, a_log, d_skip, norm_g, batch, seq):
    t = batch * seq
    L = SSD_CHUNK
    ns = seq // L
    pad = LANES - SSD_HEADS
    expand = np.kron(np.eye(LANES, SSD_HEADS), np.ones((1, SSD_HEAD_DIM)))
    expand = jnp.asarray(np.tile(expand, (3, 1)), BF16)
    blockmask = jnp.asarray(np.kron(np.eye(4), np.ones((SSD_HEAD_DIM, SSD_HEAD_DIM))), BF16)
    row = lambda b, n: b * ns + n
    const = lambda b, n: (0, 0)
    vmem = (2 * L * (SSD_CONV_CH + SSD_D_INNER + LANES) * 2 + 2 * L * SSD_D_INNER * 2
            + 2 * (6 * SSD_CONV_CH * 4 + 3 * LANES * SSD_D_INNER * 2)
            + 8 * SSD_CONV_CH * 4 + SSD_GROUPS * SSD_D_STATE * SSD_GROUP_W * 4 + (12 << 20))
    return pl.pallas_call(
        _ssd_kernel,
        grid=(batch, ns),
        in_specs=[
            pl.BlockSpec((L, SSD_D_INNER), lambda b, n: (row(b, n), SSD_X // SSD_D_INNER)),
            pl.BlockSpec((L, SSD_BC_W), lambda b, n: (row(b, n), SSD_B // SSD_BC_W)),
            pl.BlockSpec((L, SSD_BC_W), lambda b, n: (row(b, n), SSD_C // SSD_BC_W)),
            pl.BlockSpec((L, SSD_D_INNER), lambda b, n: (row(b, n), SSD_Z // SSD_D_INNER)),
            pl.BlockSpec((L, LANES), lambda b, n: (row(b, n), SSD_DT // LANES)),
            pl.BlockSpec((SSD_CONV, SSD_CONV_CH), const),
            pl.BlockSpec((1, SSD_CONV_CH), const),
            pl.BlockSpec((1, LANES), const),
            pl.BlockSpec((1, LANES), const),
            pl.BlockSpec((1, SSD_D_INNER), const),
            pl.BlockSpec((1, SSD_D_INNER), const),
            pl.BlockSpec((3 * LANES, SSD_D_INNER), const),
            pl.BlockSpec((256, 256), const),
        ],
        out_specs=pl.BlockSpec((L, SSD_D_INNER), lambda b, n: (row(b, n), 0)),
        out_shape=jax.ShapeDtypeStruct((t, SSD_D_INNER), BF16),
        scratch_shapes=[pltpu.VMEM((8, SSD_CONV_CH), F32),
                        pltpu.VMEM((SSD_GROUPS, SSD_D_STATE, SSD_GROUP_W), F32)],
        compiler_params=_compiler_params(("parallel", "arbitrary"), vmem),
        name="ssd",
    )(proj, proj, proj, proj, proj, conv_w, conv_b.reshape(1, -1),
      jnp.pad(dt_bias, (0, pad)).reshape(1, LANES), jnp.pad(a_log, (0, pad)).reshape(1, LANES),
      jnp.repeat(d_skip, SSD_HEAD_DIM).reshape(1, -1), norm_g.reshape(1, -1), expand, blockmask)


def _hyb_weight(w_in):
    w = w_in.astype(BF16)
    glr_end = HYB_SQ + GLA_GATE_RANK
    zeros = jnp.zeros((w.shape[0], HYB_W - HYB_GLR - GLA_GATE_RANK), BF16)
    return jnp.concatenate([w[:, :HYB_SQ], w[:, glr_end:], w[:, HYB_SQ:glr_end], zeros], axis=1)


def _ssd_weight(w_in):
    return jnp.pad(w_in.astype(BF16), ((0, 0), (0, SSD_W - w_in.shape[1])))


def _gla_swa_mixer(h, pos_col, norm_gain, w_in, gate_w2, gate_b, gla_norm, q_norm, k_norm, sinks, w_out, batch, seq):
    proj = _norm_matmul(h, norm_gain, _hyb_weight(w_in), tn=1536)
    w2p = jnp.pad(gate_w2, ((0, LANES - GLA_GATE_RANK), (0, 0))).astype(BF16)
    o_gla = _gla(proj, w2p, gate_b, gla_norm, batch, seq)
    o_swa = _swa(proj, pos_col, sinks, q_norm, k_norm, batch, seq)
    return _matmul_residual([o_gla, o_swa], w_out.astype(BF16), h)


def _ssd_mixer(h, norm_gain, w_in, conv_w, conv_b, dt_bias, a_log, d_skip, norm_g, w_out, batch, seq):
    proj = _norm_matmul(h, norm_gain, _ssd_weight(w_in), tn=1536)
    y = _ssd(proj, conv_w, conv_b, dt_bias, a_log, d_skip, norm_g, batch, seq)
    return _matmul_residual([y], w_out.astype(BF16), h)


def kernel(x, positions, norm_ffn, w_ffn_gu, w_ffn_down, norm_mix, hyb_w_in, gla_gate_w2, gla_gate_b, gla_norm,
           attn_q_norm, attn_k_norm, attn_sinks, hyb_w_out, ssd_w_in, ssd_conv_w, ssd_conv_b, ssd_dt_bias, ssd_a_log,
           ssd_d, ssd_norm, ssd_w_out):
    batch, seq, d = x.shape
    h = x.reshape(batch * seq, d)
    pos_col = positions.reshape(batch * seq, 1)
    depth = norm_ffn.shape[0]
    w_gu, w_down = w_ffn_gu.astype(BF16), w_ffn_down.astype(BF16)
    for layer in range(depth):
        i = layer // 2
        h = _ffn(h, norm_ffn[layer, 0], w_gu, w_down, layer, 0)
        if layer % 2 == 0:
            h = _gla_swa_mixer(h, pos_col, norm_mix[layer], hyb_w_in[i], gla_gate_w2[i], gla_gate_b[i], gla_norm[i],
                               attn_q_norm[i], attn_k_norm[i], attn_sinks[i], hyb_w_out[i], batch, seq)
        else:
            h = _ssd_mixer(h, norm_mix[layer], ssd_w_in[i], ssd_conv_w[i], ssd_conv_b[i], ssd_dt_bias[i],
                           ssd_a_log[i], ssd_d[i], ssd_norm[i], ssd_w_out[i], batch, seq)
        h = _ffn(h, norm_ffn[layer, 1], w_gu, w_down, layer, 1)
    return h.reshape(batch, seq, d)
```

```python
import functools

import numpy as np
import jax
import jax.numpy as jnp
from jax import lax
from jax.experimental import pallas as pl
from jax.experimental.pallas import tpu as pltpu

F32 = jnp.float32
BF16 = jnp.bfloat16

D_MODEL = 2048
D_FF = 5632
NORM_EPS = 1e-6
MACARON_WEIGHT = 0.5

GLA_HEADS = 4
GLA_DK = 128
GLA_DV = 256
GLA_GATE_RANK = 16
GLA_TAU = 16.0
GLA_CHUNK = 64

SWA_HEAD_DIM = 64
SWA_Q_HEADS = 16
SWA_KV_HEADS = 2
SWA_WINDOW = 128
SWA_BLOCK = 128
ROPE_THETA = 500000.0
ROPE_DIM = 16

SSD_D_INNER = 4096
SSD_HEAD_DIM = 64
SSD_HEADS = 64
SSD_GROUPS = 8
SSD_D_STATE = 128
SSD_CONV = 4
SSD_CHUNK = 64
SSD_GROUP_W = SSD_D_INNER // SSD_GROUPS
SSD_BC_W = SSD_GROUPS * SSD_D_STATE
SSD_CONV_CH = SSD_D_INNER + 2 * SSD_BC_W

LANES = 128
VMEM_LIMIT_CAP = 56 * 1024 * 1024

HYB_Q, HYB_K, HYB_V, HYB_R, HYB_SQ, HYB_SK, HYB_SV, HYB_GLR = 0, 512, 1024, 2048, 3072, 4096, 4224, 4352
HYB_W = 4608
SSD_Z, SSD_X, SSD_B, SSD_C, SSD_DT = 0, 4096, 8192, 9216, 10240
SSD_W = 10752


def _compiler_params(semantics, vmem_bytes):
    limit = min(int(vmem_bytes * 1.25) + (4 << 20), VMEM_LIMIT_CAP)
    return pltpu.CompilerParams(dimension_semantics=semantics, vmem_limit_bytes=limit)


def _rms_norm(x, gain):
    ms = jnp.mean(x * x, axis=-1, keepdims=True)
    return x * lax.rsqrt(ms + NORM_EPS) * gain


def _silu(x):
    return x * jax.nn.sigmoid(x)


def _softplus(x):
    return jnp.maximum(x, 0.0) + jnp.log1p(jnp.exp(-jnp.abs(x)))


def _split_bf16(x, n):
    parts, rest = [], x
    for _ in range(n):
        p = rest.astype(BF16)
        parts.append(p)
        rest = rest - p.astype(F32)
    return parts


def _dot(a, b):
    return jnp.dot(a, b, preferred_element_type=F32)


def _dot_nt(a, b):
    return lax.dot_general(a, b, (((1,), (1,)), ((), ())), preferred_element_type=F32)


def _select_dot_lhs(sel, x, n):
    return sum(_dot(sel, p) for p in _split_bf16(x, n))


def _select_dot_rhs(x, sel, n):
    return sum(_dot(p, sel) for p in _split_bf16(x, n))


def _ffn_kernel(x_ref, g_ref, wg_ref, wu_ref, wd_ref, o_ref, hn_ref):
    @pl.when(pl.program_id(1) == 0)
    def _():
        x = x_ref[...]
        hn_ref[...] = _rms_norm(x, g_ref[...]).astype(BF16)
        o_ref[...] = x

    h = hn_ref[...]
    gate = _dot(h, wg_ref[...])
    up = _dot(h, wu_ref[...])
    act = (_silu(gate) * (up * MACARON_WEIGHT)).astype(BF16)
    o_ref[...] += _dot(act, wd_ref[...])


def _ffn(h, gain, w_gu, w_down, layer, pos, tm=1024, tf=512):
    t, d = h.shape
    nf = D_FF // tf
    vmem = 2 * (2 * tm * d * 4) + 2 * 3 * d * tf * 2 + tm * d * 2 + 3 * tm * tf * 4
    return pl.pallas_call(
        _ffn_kernel,
        grid=(t // tm, nf),
        in_specs=[
            pl.BlockSpec((tm, d), lambda i, j: (i, 0)),
            pl.BlockSpec((1, d), lambda i, j: (0, 0)),
            pl.BlockSpec((None, None, d, tf), lambda i, j: (layer, pos, 0, j)),
            pl.BlockSpec((None, None, d, tf), lambda i, j: (layer, pos, 0, j + nf)),
            pl.BlockSpec((None, None, tf, d), lambda i, j: (layer, pos, j, 0)),
        ],
        out_specs=pl.BlockSpec((tm, d), lambda i, j: (i, 0)),
        out_shape=jax.ShapeDtypeStruct((t, d), F32),
        scratch_shapes=[pltpu.VMEM((tm, d), BF16)],
        compiler_params=_compiler_params(("parallel", "arbitrary"), vmem),
        name="ffn",
    )(h, gain.reshape(1, d), w_gu, w_gu, w_down)


def _norm_matmul_kernel(x_ref, g_ref, w_ref, o_ref, hn_ref):
    @pl.when(pl.program_id(1) == 0)
    def _():
        hn_ref[...] = _rms_norm(x_ref[...], g_ref[...]).astype(BF16)

    o_ref[...] = _dot(hn_ref[...], w_ref[...]).astype(o_ref.dtype)


def _norm_matmul(h, gain, w, tn, tm=1024):
    t, d = h.shape
    n = w.shape[1]
    vmem = 2 * tm * d * 4 + 2 * d * tn * 2 + 2 * tm * tn * 2 + tm * d * 2 + tm * tn * 4
    return pl.pallas_call(
        _norm_matmul_kernel,
        grid=(t // tm, n // tn),
        in_specs=[
            pl.BlockSpec((tm, d), lambda i, j: (i, 0)),
            pl.BlockSpec((1, d), lambda i, j: (0, 0)),
            pl.BlockSpec((d, tn), lambda i, j: (0, j)),
        ],
        out_specs=pl.BlockSpec((tm, tn), lambda i, j: (i, j)),
        out_shape=jax.ShapeDtypeStruct((t, n), BF16),
        scratch_shapes=[pltpu.VMEM((tm, d), BF16)],
        compiler_params=_compiler_params(("parallel", "arbitrary"), vmem),
        name="norm_matmul",
    )(h, gain.reshape(1, d), w)


def _matmul_residual_kernel(*refs, n_in):
    a_refs, w_refs = refs[:n_in], refs[n_in:2 * n_in]
    r_ref, o_ref = refs[2 * n_in], refs[2 * n_in + 1]
    acc = r_ref[...]
    for a_ref, w_ref in zip(a_refs, w_refs):
        acc = acc + _dot(a_ref[...], w_ref[...])
    o_ref[...] = acc


def _matmul_residual(acts, w, res, tm=512):
    t, d = res.shape
    n_in = len(acts)
    k = acts[0].shape[1]
    assert all(a.shape[1] == k for a in acts) and w.shape == (n_in * k, d)
    vmem = 2 * 2 * tm * d * 4 + n_in * (2 * tm * k * 2 + 2 * k * d * 2)
    in_specs = [pl.BlockSpec((tm, k), lambda i: (i, 0)) for _ in acts]
    in_specs += [pl.BlockSpec((k, d), functools.partial(lambda i, idx: (idx, 0), idx=idx)) for idx in range(n_in)]
    weights = [w] * n_in
    in_specs.append(pl.BlockSpec((tm, d), lambda i: (i, 0)))
    return pl.pallas_call(
        functools.partial(_matmul_residual_kernel, n_in=n_in),
        grid=(t // tm,),
        in_specs=in_specs,
        out_specs=pl.BlockSpec((tm, d), lambda i: (i, 0)),
        out_shape=jax.ShapeDtypeStruct((t, d), F32),
        compiler_params=_compiler_params(("parallel",), vmem),
        name="matmul_residual",
    )(*acts, *weights, res)


def _gla_kernel(q_ref, k_ref, v_ref, r_ref, glr_ref, w2_ref, gb_ref, gn_ref, o_ref, st_ref, *, n_chunks):
    @pl.when(pl.program_id(1) == 0)
    def _():
        st_ref[...] = jnp.zeros_like(st_ref)

    c = GLA_CHUNK
    lb = n_chunks * c
    ri = lax.broadcasted_iota(jnp.int32, (lb, lb), 0)
    ci = lax.broadcasted_iota(jnp.int32, (lb, lb), 1)
    same_chunk_tril = (ri >= ci) & ((ri // c) == (ci // c))
    tril = jnp.where(same_chunk_tril, 1.0, 0.0).astype(BF16)
    causal = lax.broadcasted_iota(jnp.int32, (c, c), 0) >= lax.broadcasted_iota(jnp.int32, (c, c), 1)
    gn = gn_ref[...]

    z = _dot(glr_ref[...], w2_ref[...]) + gb_ref[...]
    log_a = (jnp.minimum(z, 0.0) - jnp.log1p(jnp.exp(-jnp.abs(z)))) * (1.0 / GLA_TAU)
    bcum = _select_dot_lhs(tril, log_a, 3)
    q_all = q_ref[...].astype(F32) * (GLA_DK ** -0.5) * jnp.exp(bcum)
    k_all = k_ref[...].astype(F32)
    k_in_all = k_all * jnp.exp(-bcum)

    items = [(h, idx) for h in range(GLA_HEADS) for idx in range(n_chunks)]
    kcols = lambda h: slice(h * GLA_DK, (h + 1) * GLA_DK)
    vcols = lambda h: slice(h * GLA_DV, (h + 1) * GLA_DV)
    rows = lambda idx: slice(idx * c, (idx + 1) * c)
    q_in, k_in, k_end, decay_end, v, v_t = {}, {}, {}, {}, {}, {}
    for h, idx in items:
        b_c = bcum[rows(idx), kcols(h)]
        b_last = b_c[c - 1:c, :]
        q_in[h, idx] = q_all[rows(idx), kcols(h)].astype(BF16)
        k_in[h, idx] = k_in_all[rows(idx), kcols(h)].astype(BF16)
        k_end[h, idx] = (k_all[rows(idx), kcols(h)] * jnp.exp(b_last - b_c)).astype(BF16)
        decay_end[h, idx] = jnp.exp(b_last)
        v[h, idx] = v_ref[rows(idx), vcols(h)]
        v_t[h, idx] = v[h, idx].astype(F32).T.astype(BF16)
    attn = {it: jnp.where(causal, _dot_nt(q_in[it], k_in[it]), 0.0).astype(BF16) for it in items}
    o_intra = {it: _dot(attn[it], v[it]) for it in items}
    upd = {it: _dot(v_t[it], k_end[it]) for it in items}
    for h in range(GLA_HEADS):
        state = st_ref[h]
        for idx in range(n_chunks):
            o = o_intra[h, idx] + _dot_nt(q_in[h, idx], state.astype(BF16))
            state = decay_end[h, idx] * state + upd[h, idx]
            o = _rms_norm(o, gn) * _silu(r_ref[rows(idx), vcols(h)].astype(F32))
            o_ref[rows(idx), vcols(h)] = o.astype(o_ref.dtype)
        st_ref[h] = state


def _gla(proj, w2p, gate_b, gla_norm, batch, seq, lb=256):
    t = batch * seq
    ns = seq // lb
    qk_w, v_w = GLA_HEADS * GLA_DK, GLA_HEADS * GLA_DV
    row = lambda b, n: b * ns + n
    vmem = 2 * lb * (2 * qk_w + 2 * v_w + LANES) * 2 + 2 * lb * v_w * 2 + 8 * lb * qk_w * 4 + (4 << 20)
    return pl.pallas_call(
        functools.partial(_gla_kernel, n_chunks=lb // GLA_CHUNK),
        grid=(batch, ns),
        in_specs=[
            pl.BlockSpec((lb, qk_w), lambda b, n: (row(b, n), HYB_Q // qk_w)),
            pl.BlockSpec((lb, qk_w), lambda b, n: (row(b, n), HYB_K // qk_w)),
            pl.BlockSpec((lb, v_w), lambda b, n: (row(b, n), HYB_V // v_w)),
            pl.BlockSpec((lb, v_w), lambda b, n: (row(b, n), HYB_R // v_w)),
            pl.BlockSpec((lb, LANES), lambda b, n: (row(b, n), HYB_GLR // LANES)),
            pl.BlockSpec((LANES, qk_w), lambda b, n: (0, 0)),
            pl.BlockSpec((1, qk_w), lambda b, n: (0, 0)),
            pl.BlockSpec((1, GLA_DV), lambda b, n: (0, 0)),
        ],
        out_specs=pl.BlockSpec((lb, v_w), lambda b, n: (row(b, n), 0)),
        out_shape=jax.ShapeDtypeStruct((t, v_w), BF16),
        scratch_shapes=[pltpu.VMEM((GLA_HEADS, GLA_DV, GLA_DK), F32)],
        compiler_params=_compiler_params(("parallel", "arbitrary"), vmem),
        name="gla",
    )(proj, proj, proj, proj, proj, w2p, gate_b.reshape(1, -1), gla_norm.reshape(1, -1))


def _swa_kernel(sink_ref, pos_ref, q_ref, kv_ref, qn_ref, kn_ref, freq_ref, bd_ref, o_ref, kprev_ref, vprev_ref):
    n = pl.program_id(1)
    w = SWA_BLOCK
    lane = lax.broadcasted_iota(jnp.int32, (w, LANES), 1)
    lane_lo = lane < SWA_HEAD_DIM
    lane_lo_kv = lax.broadcasted_iota(jnp.int32, (2 * w, LANES), 1) < SWA_HEAD_DIM
    dim = lane & (SWA_HEAD_DIM - 1)
    half = ROPE_DIM // 2
    bd = bd_ref[...]

    @pl.when(n == 0)
    def _():
        kprev_ref[...] = jnp.zeros_like(kprev_ref)
        vprev_ref[...] = jnp.zeros_like(vprev_ref)

    ang = pos_ref[...].astype(F32) * freq_ref[...]
    cos, sin = jnp.cos(ang), jnp.sin(ang)
    s_hi = jnp.where((dim >= half) & (dim < ROPE_DIM), sin, 0.0)
    s_lo = jnp.where(dim < half, -sin, 0.0)

    def norm_rope(x, gain):
        ms = _dot((x * x).astype(BF16), bd) * (1.0 / SWA_HEAD_DIM)
        xn = x * lax.rsqrt(ms + NORM_EPS) * gain
        return xn * cos + pltpu.roll(xn, half, 1) * s_hi + pltpu.roll(xn, LANES - half, 1) * s_lo

    k_cur = norm_rope(kv_ref[:, 0:LANES].astype(F32), kn_ref[...])
    v_cur = kv_ref[:, LANES:2 * LANES].astype(F32)
    kk = jnp.concatenate([kprev_ref[...], k_cur], axis=0)
    vv = jnp.concatenate([vprev_ref[...], v_cur], axis=0)
    kprev_ref[...] = k_cur
    vprev_ref[...] = v_cur
    kk_sw = pltpu.roll(kk, SWA_HEAD_DIM, 1)
    vv_sw = pltpu.roll(vv, SWA_HEAD_DIM, 1)
    k2 = [jnp.where(lane_lo_kv, kk, kk_sw).astype(BF16), jnp.where(lane_lo_kv, kk_sw, kk).astype(BF16)]
    v2 = [jnp.where(lane_lo_kv, vv, vv_sw).astype(BF16), jnp.where(lane_lo_kv, vv_sw, vv).astype(BF16)]

    qi = lax.broadcasted_iota(jnp.int32, (w, 2 * w), 0)
    ki = lax.broadcasted_iota(jnp.int32, (w, 2 * w), 1)
    rel = qi + w - ki
    valid = (rel >= 0) & (rel < SWA_WINDOW) & ((ki >= w) | (n > 0))

    qn = qn_ref[...] * (SWA_HEAD_DIM ** -0.5)
    heads_per_kv = SWA_Q_HEADS // SWA_KV_HEADS
    pairs = range(SWA_Q_HEADS // 2)
    heads = [(pair, hf) for pair in pairs for hf in range(2)]
    qp = [norm_rope(q_ref[:, pl.ds(pair * LANES, LANES)].astype(F32), qn) for pair in pairs]
    scores = []
    for pair, hf in heads:
        keep = lane_lo if hf == 0 else jnp.logical_not(lane_lo)
        qm = jnp.where(keep, qp[pair], 0.0).astype(BF16)
        scores.append(jnp.where(valid, _dot_nt(qm, k2[(2 * pair) // heads_per_kv]), -1e30))
    probs, rdenom = [], []
    for (pair, hf), s in zip(heads, scores):
        sink = sink_ref[2 * pair + hf]
        m = jnp.maximum(jnp.max(s, axis=-1, keepdims=True), sink)
        p = jnp.exp(s - m)
        rdenom.append(1.0 / (jnp.sum(p, axis=-1, keepdims=True) + jnp.exp(sink - m)))
        probs.append(p.astype(BF16))
    outs = [_dot(p, v2[(2 * pair) // heads_per_kv]) * r for (pair, hf), p, r in zip(heads, probs, rdenom)]
    for pair in pairs:
        o_ref[:, pl.ds(pair * LANES, LANES)] = jnp.where(lane_lo, outs[2 * pair], outs[2 * pair + 1]).astype(o_ref.dtype)


def _swa(proj, pos_col, sinks, q_norm, k_norm, batch, seq):
    t = batch * seq
    w = SWA_BLOCK
    nb = seq // w
    half = ROPE_DIM // 2
    inv_freq = ROPE_THETA ** (-2.0 * jnp.arange(half, dtype=F32) / ROPE_DIM)
    dim = np.arange(LANES) % SWA_HEAD_DIM
    freq_row = jnp.where(dim < ROPE_DIM, inv_freq[dim % half], 0.0).reshape(1, LANES).astype(F32)
    blockdiag = jnp.asarray(np.kron(np.eye(LANES // SWA_HEAD_DIM), np.ones((SWA_HEAD_DIM, SWA_HEAD_DIM))), BF16)
    cur = lambda b, n: b * nb + n
    q_w = SWA_Q_HEADS * SWA_HEAD_DIM
    vmem = 2 * w * (q_w + 2 * LANES) * 2 + 2 * w * q_w * 2 + 4 * w * LANES * 4 + (8 << 20)
    return pl.pallas_call(
        _swa_kernel,
        grid=(batch, nb),
        in_specs=[
            pl.BlockSpec(memory_space=pltpu.SMEM),
            pl.BlockSpec((w, 1), lambda b, n: (cur(b, n), 0)),
            pl.BlockSpec((w, q_w), lambda b, n: (cur(b, n), HYB_SQ // q_w)),
            pl.BlockSpec((w, 2 * LANES), lambda b, n: (cur(b, n), HYB_SK // (2 * LANES))),
            pl.BlockSpec((1, LANES), lambda b, n: (0, 0)),
            pl.BlockSpec((1, LANES), lambda b, n: (0, 0)),
            pl.BlockSpec((1, LANES), lambda b, n: (0, 0)),
            pl.BlockSpec((LANES, LANES), lambda b, n: (0, 0)),
        ],
        out_specs=pl.BlockSpec((w, q_w), lambda b, n: (cur(b, n), 0)),
        out_shape=jax.ShapeDtypeStruct((t, q_w), BF16),
        scratch_shapes=[pltpu.VMEM((w, LANES), F32), pltpu.VMEM((w, LANES), F32)],
        compiler_params=_compiler_params(("parallel", "arbitrary"), vmem),
        name="swa",
    )(sinks.astype(F32), pos_col, proj, proj,
      jnp.tile(q_norm, 2).reshape(1, LANES), jnp.tile(k_norm, 2).reshape(1, LANES), freq_row, blockdiag)


def _ssd_kernel(x_ref, b_ref, c_ref, z_ref, dt_ref, cw_ref, cb_ref, dtb_ref, alog_ref, dsk_ref, ng_ref, e_ref, bm_ref,
                o_ref, tail_ref, st_ref, *, n_chunks):
    n = pl.program_id(1)
    L = SSD_CHUNK
    lb = n_chunks * L
    gw = SSD_GROUP_W
    hp = SSD_HEAD_DIM
    ns = SSD_D_STATE
    groups = range(SSD_GROUPS)
    chunks = range(n_chunks)
    crow = lambda c: slice(c * L, (c + 1) * L)

    @pl.when(n == 0)
    def _():
        tail_ref[...] = jnp.zeros_like(tail_ref)
        st_ref[...] = jnp.zeros_like(st_ref)

    cur = jnp.concatenate([x_ref[...], b_ref[...], c_ref[...]], axis=1).astype(F32)
    ext = jnp.concatenate([tail_ref[...], cur], axis=0)
    tail_ref[...] = cur[lb - 8:lb, :]
    acc = cb_ref[...] + cw_ref[SSD_CONV - 1:SSD_CONV, :] * cur
    for j in range(SSD_CONV - 1):
        shifted = pltpu.roll(ext, SSD_CONV - 1 - j, 0)[8:8 + lb, :]
        acc = acc + cw_ref[j:j + 1, :] * shifted
    xbc = _silu(acc)

    lane = lax.broadcasted_iota(jnp.int32, (1, LANES), 1)
    dt = _softplus(dt_ref[...].astype(F32) + dtb_ref[...])
    a_neg = jnp.where(lane < SSD_HEADS, -jnp.exp(alog_ref[...]), 0.0)
    ri = lax.broadcasted_iota(jnp.int32, (lb, lb), 0)
    ci = lax.broadcasted_iota(jnp.int32, (lb, lb), 1)
    tril = jnp.where((ri >= ci) & ((ri // L) == (ci // L)), 1.0, 0.0).astype(BF16)
    acum = _select_dot_lhs(tril, dt * a_neg, 3)
    wgt_small = jnp.concatenate([jnp.exp(acum[c * L + L - 1:c * L + L, :] - acum[crow(c), :]) for c in chunks], axis=0)
    acum_cat = jnp.concatenate(_split_bf16(acum, 3), axis=1)
    dt_cat = jnp.concatenate(_split_bf16(dt, 2), axis=1)
    eacum_cat = jnp.concatenate(_split_bf16(jnp.exp(acum), 2), axis=1)
    wgt_cat = jnp.concatenate(_split_bf16(wgt_small, 2), axis=1)
    acum_t = [jnp.concatenate([acum[crow(c), :], acum[crow(c), :]], axis=0).T for c in chunks]

    lane_lo = lane < hp
    row_l = lax.broadcasted_iota(jnp.int32, (L, gw), 0)
    src_s = lax.broadcasted_iota(jnp.int32, (L, gw), 1) & (hp - 1)
    causal = src_s <= row_l
    blockmask = bm_ref[...]
    heads_per_group = SSD_HEADS // SSD_GROUPS

    xs, b_bf, c_bf = [], [], []
    for g in groups:
        xs.append(xbc[:, g * gw:(g + 1) * gw])
        b_bf.append(xbc[:, SSD_D_INNER + g * ns:SSD_D_INNER + (g + 1) * ns])
        c_bf.append(xbc[:, SSD_D_INNER + SSD_BC_W + g * ns:SSD_D_INNER + SSD_BC_W + (g + 1) * ns].astype(BF16))
    b_t = [[b_bf[g][crow(c), :].T.astype(BF16) for g in groups] for c in chunks]
    b_bf = [b.astype(BF16) for b in b_bf]

    col, dt_exp, ecol, wgt = [], [], [], []
    for g in groups:
        e_g = e_ref[:, pl.ds(g * gw, gw)]
        col.append(_dot(acum_cat, e_g))
        dt_exp.append(_dot(dt_cat, e_g[:2 * LANES, :]))
        ecol.append(_dot(eacum_cat, e_g[:2 * LANES, :]))
        wgt.append(_dot(wgt_cat, e_g[:2 * LANES, :]))
    cb = [[_dot_nt(c_bf[g][crow(c), :], b_bf[g][crow(c), :]) for g in groups] for c in chunks]

    xdt = [xs[g] * dt_exp[g] for g in groups]
    xdt_bf = [t.astype(BF16) for t in xdt]
    xw = [(xdt[g] * wgt[g]).astype(BF16) for g in groups]
    m = []
    for c in chunks:
        m_c = []
        for g in groups:
            row_parts = []
            for pr in range(heads_per_group // 2):
                h0 = g * heads_per_group + 2 * pr
                r = jnp.where(lane_lo, acum_t[c][h0:h0 + 1, :], acum_t[c][h0 + 1:h0 + 2, :])
                row_parts.append(jnp.broadcast_to(r, (L, LANES)))
            row = jnp.concatenate(row_parts, axis=1)
            decay = jnp.where(causal, jnp.exp(col[g][crow(c), :] - row), 0.0)
            cb2 = jnp.concatenate([cb[c][g], cb[c][g]], axis=1)
            cb_exp = jnp.concatenate([cb2] * (gw // LANES), axis=1)
            m_c.append((cb_exp * decay).astype(BF16))
        m.append(m_c)

    y_diag, upd = [], []
    for c in chunks:
        yd_c, upd_c = [], []
        for g in groups:
            y_parts = []
            for qd in range(gw // 256):
                sl = slice(qd * 256, (qd + 1) * 256)
                rhs = jnp.concatenate([xdt_bf[g][crow(c), sl]] * 4, axis=0) * blockmask
                y_parts.append(_dot(m[c][g][:, sl], rhs))
            yd_c.append(jnp.concatenate(y_parts, axis=1))
            upd_c.append(_dot(b_t[c][g], xw[g][crow(c), :]))
        y_diag.append(yd_c)
        upd.append(upd_c)

    for g in groups:
        cols = pl.ds(g * gw, gw)
        state = st_ref[g]
        y_g = []
        for c in chunks:
            y_off = _dot(c_bf[g][crow(c), :], state.astype(BF16)) * ecol[g][crow(c), :]
            y_g.append(y_diag[c][g] + y_off)
            state = ecol[g][c * L + L - 1:c * L + L, :] * state + upd[c][g]
        st_ref[g] = state
        y = jnp.concatenate(y_g, axis=0) + xs[g] * dsk_ref[:, cols]
        y = y * _silu(z_ref[:, cols].astype(F32))
        o_ref[:, cols] = _rms_norm(y, ng_ref[:, cols]).astype(o_ref.dtype)


def _ssd(proj, conv_w, conv_b, dt_bias, a_log, d_skip, norm_g, batch, seq, n_chunks=2):
    t = batch * seq
    L = n_chunks * SSD_CHUNK
    ns = seq // L
    pad = LANES - SSD_HEADS
    expand = np.kron(np.eye(LANES, SSD_HEADS), np.ones((1, SSD_HEAD_DIM)))
    expand = jnp.asarray(np.tile(expand, (3, 1)), BF16)
    blockmask = jnp.asarray(np.kron(np.eye(4), np.ones((SSD_HEAD_DIM, SSD_HEAD_DIM))), BF16)
    row = lambda b, n: b * ns + n
    const = lambda b, n: (0, 0)
    vmem = (2 * L * (SSD_CONV_CH + SSD_D_INNER + LANES) * 2 + 2 * L * SSD_D_INNER * 2
            + 2 * (6 * SSD_CONV_CH * 4 + 3 * LANES * SSD_D_INNER * 2)
            + 8 * SSD_CONV_CH * 4 + SSD_GROUPS * SSD_D_STATE * SSD_GROUP_W * 4 + (12 << 20))
    return pl.pallas_call(
        functools.partial(_ssd_kernel, n_chunks=n_chunks),
        grid=(batch, ns),
        in_specs=[
            pl.BlockSpec((L, SSD_D_INNER), lambda b, n: (row(b, n), SSD_X // SSD_D_INNER)),
            pl.BlockSpec((L, SSD_BC_W), lambda b, n: (row(b, n), SSD_B // SSD_BC_W)),
            pl.BlockSpec((L, SSD_BC_W), lambda b, n: (row(b, n), SSD_C // SSD_BC_W)),
            pl.BlockSpec((L, SSD_D_INNER), lambda b, n: (row(b, n), SSD_Z // SSD_D_INNER)),
            pl.BlockSpec((L, LANES), lambda b, n: (row(b, n), SSD_DT // LANES)),
            pl.BlockSpec((SSD_CONV, SSD_CONV_CH), const),
            pl.BlockSpec((1, SSD_CONV_CH), const),
            pl.BlockSpec((1, LANES), const),
            pl.BlockSpec((1, LANES), const),
            pl.BlockSpec((1, SSD_D_INNER), const),
            pl.BlockSpec((1, SSD_D_INNER), const),
            pl.BlockSpec((3 * LANES, SSD_D_INNER), const),
            pl.BlockSpec((256, 256), const),
        ],
        out_specs=pl.BlockSpec((L, SSD_D_INNER), lambda b, n: (row(b, n), 0)),
        out_shape=jax.ShapeDtypeStruct((t, SSD_D_INNER), BF16),
        scratch_shapes=[pltpu.VMEM((8, SSD_CONV_CH), F32),
                        pltpu.VMEM((SSD_GROUPS, SSD_D_STATE, SSD_GROUP_W), F32)],
        compiler_params=_compiler_params(("parallel", "arbitrary"), vmem),
        name="ssd",
    )(proj, proj, proj, proj, proj, conv_w, conv_b.reshape(1, -1),
      jnp.pad(dt_bias, (0, pad)).reshape(1, LANES), jnp.pad(a_log, (0, pad)).reshape(1, LANES),
      jnp.repeat(d_skip, SSD_HEAD_DIM).reshape(1, -1), norm_g.reshape(1, -1), expand, blockmask)


def _hyb_weight(w_in):
    w = w_in.astype(BF16)
    glr_end = HYB_SQ + GLA_GATE_RANK
    zeros = jnp.zeros((w.shape[0], HYB_W - HYB_GLR - GLA_GATE_RANK), BF16)
    return jnp.concatenate([w[:, :HYB_SQ], w[:, glr_end:], w[:, HYB_SQ:glr_end], zeros], axis=1)


def _ssd_weight(w_in):
    return jnp.pad(w_in, ((0, 0), (0, SSD_W - w_in.shape[1]))).astype(BF16)


def _gla_swa_mixer(h, pos_col, norm_gain, w_in, gate_w2, gate_b, gla_norm, q_norm, k_norm, sinks, w_out, batch, seq):
    proj = _norm_matmul(h, norm_gain, _hyb_weight(w_in), tn=1536)
    w2p = jnp.pad(gate_w2, ((0, LANES - GLA_GATE_RANK), (0, 0))).astype(BF16)
    o_gla = _gla(proj, w2p, gate_b, gla_norm, batch, seq)
    o_swa = _swa(proj, pos_col, sinks, q_norm, k_norm, batch, seq)
    return _matmul_residual([o_gla, o_swa], w_out.astype(BF16), h)


def _ssd_mixer(h, norm_gain, w_in, conv_w, conv_b, dt_bias, a_log, d_skip, norm_g, w_out, batch, seq):
    proj = _norm_matmul(h, norm_gain, _ssd_weight(w_in), tn=1536)
    y = _ssd(proj, conv_w, conv_b, dt_bias, a_log, d_skip, norm_g, batch, seq)
    return _matmul_residual([y], w_out.astype(BF16), h)


def kernel(x, positions, norm_ffn, w_ffn_gu, w_ffn_down, norm_mix, hyb_w_in, gla_gate_w2, gla_gate_b, gla_norm,
           attn_q_norm, attn_k_norm, attn_sinks, hyb_w_out, ssd_w_in, ssd_conv_w, ssd_conv_b, ssd_dt_bias, ssd_a_log,
           ssd_d, ssd_norm, ssd_w_out):
    batch, seq, d = x.shape
    h = x.reshape(batch * seq, d)
    pos_col = positions.reshape(batch * seq, 1)
    depth = norm_ffn.shape[0]
    w_gu, w_down = w_ffn_gu.astype(BF16), w_ffn_down.astype(BF16)
    for layer in range(depth):
        i = layer // 2
        h = _ffn(h, norm_ffn[layer, 0], w_gu, w_down, layer, 0)
        if layer % 2 == 0:
            h = _gla_swa_mixer(h, pos_col, norm_mix[layer], hyb_w_in[i], gla_gate_w2[i], gla_gate_b[i], gla_norm[i],
                               attn_q_norm[i], attn_k_norm[i], attn_sinks[i], hyb_w_out[i], batch, seq)
        else:
            h = _ssd_mixer(h, norm_mix[layer], ssd_w_in[i], ssd_conv_w[i], ssd_conv_b[i], ssd_dt_bias[i],
                           ssd_a_log[i], ssd_d[i], ssd_norm[i], ssd_w_out[i], batch, seq)
        h = _ffn(h, norm_ffn[layer, 1], w_gu, w_down, layer, 1)
    return h.reshape(batch, seq, d)
```

```python
import functools

import numpy as np
import jax
import jax.numpy as jnp
from jax import lax
from jax.experimental import pallas as pl
from jax.experimental.pallas import tpu as pltpu

F32 = jnp.float32
BF16 = jnp.bfloat16

D_MODEL = 2048
D_FF = 5632
NORM_EPS = 1e-6
MACARON_WEIGHT = 0.5

GLA_HEADS = 4
GLA_DK = 128
GLA_DV = 256
GLA_GATE_RANK = 16
GLA_TAU = 16.0
GLA_CHUNK = 64

SWA_HEAD_DIM = 64
SWA_Q_HEADS = 16
SWA_KV_HEADS = 2
SWA_WINDOW = 128
SWA_BLOCK = 128
ROPE_THETA = 500000.0
ROPE_DIM = 16

SSD_D_INNER = 4096
SSD_HEAD_DIM = 64
SSD_HEADS = 64
SSD_GROUPS = 8
SSD_D_STATE = 128
SSD_CONV = 4
SSD_CHUNK = 64
SSD_GROUP_W = SSD_D_INNER // SSD_GROUPS
SSD_BC_W = SSD_GROUPS * SSD_D_STATE
SSD_CONV_CH = SSD_D_INNER + 2 * SSD_BC_W

LANES = 128
VMEM_LIMIT_CAP = 60 * 1024 * 1024

HYB_Q, HYB_K, HYB_V, HYB_R, HYB_SQ, HYB_SK, HYB_SV, HYB_GLR = 0, 512, 1024, 2048, 3072, 4096, 4224, 4352
HYB_W = 4608
SSD_Z, SSD_X, SSD_B, SSD_C, SSD_DT = 0, 4096, 8192, 9216, 10240
SSD_W = 10752


def _compiler_params(semantics, vmem_bytes):
    limit = min(int(vmem_bytes * 1.25) + (4 << 20), VMEM_LIMIT_CAP)
    return pltpu.CompilerParams(dimension_semantics=semantics, vmem_limit_bytes=limit)


def _rms_norm(x, gain):
    ms = jnp.mean(x * x, axis=-1, keepdims=True)
    return x * lax.rsqrt(ms + NORM_EPS) * gain


def _silu(x):
    return x * jax.nn.sigmoid(x)


def _softplus(x):
    return jnp.maximum(x, 0.0) + jnp.log1p(jnp.exp(-jnp.abs(x)))


def _split_bf16(x, n):
    parts, rest = [], x
    for _ in range(n):
        p = rest.astype(BF16)
        parts.append(p)
        rest = rest - p.astype(F32)
    return parts


def _dot(a, b):
    return jnp.dot(a, b, preferred_element_type=F32)


def _dot_nt(a, b):
    return lax.dot_general(a, b, (((1,), (1,)), ((), ())), preferred_element_type=F32)


def _select_dot_lhs(sel, x, n):
    return sum(_dot(sel, p) for p in _split_bf16(x, n))


def _select_dot_rhs(x, sel, n):
    return sum(_dot(p, sel) for p in _split_bf16(x, n))


def _ffn_kernel(x_ref, g_ref, wg_ref, wu_ref, wd_ref, *rest, cast_next):
    if cast_next:
        ngu_ref, nd_ref, o_ref, ogu_ref, od_ref, hn_ref = rest
    else:
        o_ref, hn_ref = rest

    @pl.when(pl.program_id(1) == 0)
    def _():
        x = x_ref[...]
        hn_ref[...] = _rms_norm(x, g_ref[...]).astype(BF16)
        o_ref[...] = x

    h = hn_ref[...]
    gate = _dot(h, wg_ref[...])
    up = _dot(h, wu_ref[...])
    act = (_silu(gate) * (up * MACARON_WEIGHT)).astype(BF16)
    o_ref[...] += _dot(act, wd_ref[...])
    if cast_next:
        ogu_ref[...] = ngu_ref[...].astype(BF16)
        od_ref[...] = nd_ref[...].astype(BF16)


def _ffn(h, gain, w_gu, w_down, next_weights=None, tm=1024, tf=512):
    t, d = h.shape
    nf = D_FF // tf
    ni = t // tm
    vmem = 2 * (2 * tm * d * 4) + 2 * 3 * d * tf * 2 + tm * d * 2 + 3 * tm * tf * 4
    in_specs = [
        pl.BlockSpec((tm, d), lambda i, j: (i, 0)),
        pl.BlockSpec((1, d), lambda i, j: (0, 0)),
        pl.BlockSpec((d, tf), lambda i, j: (0, j)),
        pl.BlockSpec((d, tf), lambda i, j: (0, j + nf)),
        pl.BlockSpec((tf, d), lambda i, j: (j, 0)),
    ]
    out_specs = [pl.BlockSpec((tm, d), lambda i, j: (i, 0))]
    out_shape = [jax.ShapeDtypeStruct((t, d), F32)]
    args = [h, gain.reshape(1, d), w_gu, w_gu, w_down]
    if next_weights is not None:
        w_gu_all, w_down_all, layer, pos = next_weights
        gu_blk = (d // ni, 2 * D_FF // nf)
        dn_blk = (D_FF // nf, d // ni)
        in_specs += [
            pl.BlockSpec((None, None) + gu_blk, lambda i, j: (layer, pos, i, j)),
            pl.BlockSpec((None, None) + dn_blk, lambda i, j: (layer, pos, j, i)),
        ]
        out_specs += [pl.BlockSpec(gu_blk, lambda i, j: (i, j)), pl.BlockSpec(dn_blk, lambda i, j: (j, i))]
        out_shape += [jax.ShapeDtypeStruct((d, 2 * D_FF), BF16), jax.ShapeDtypeStruct((D_FF, d), BF16)]
        args += [w_gu_all, w_down_all]
        vmem += 2 * (gu_blk[0] * gu_blk[1] + dn_blk[0] * dn_blk[1]) * (4 + 2)
    outs = pl.pallas_call(
        functools.partial(_ffn_kernel, cast_next=next_weights is not None),
        grid=(ni, nf),
        in_specs=in_specs,
        out_specs=out_specs,
        out_shape=out_shape,
        scratch_shapes=[pltpu.VMEM((tm, d), BF16)],
        compiler_params=_compiler_params(("parallel", "arbitrary"), vmem),
        name="ffn",
    )(*args)
    return outs[0], (tuple(outs[1:]) if next_weights is not None else None)


def _norm_matmul_kernel(x_ref, g_ref, w_ref, o_ref, hn_ref):
    @pl.when(pl.program_id(1) == 0)
    def _():
        hn_ref[...] = _rms_norm(x_ref[...], g_ref[...]).astype(BF16)

    o_ref[...] = _dot(hn_ref[...], w_ref[...]).astype(o_ref.dtype)


def _norm_matmul(h, gain, w, tn, tm=1024):
    t, d = h.shape
    n = w.shape[1]
    vmem = 2 * tm * d * 4 + 2 * d * tn * 2 + 2 * tm * tn * 2 + tm * d * 2 + tm * tn * 4
    return pl.pallas_call(
        _norm_matmul_kernel,
        grid=(t // tm, n // tn),
        in_specs=[
            pl.BlockSpec((tm, d), lambda i, j: (i, 0)),
            pl.BlockSpec((1, d), lambda i, j: (0, 0)),
            pl.BlockSpec((d, tn), lambda i, j: (0, j)),
        ],
        out_specs=pl.BlockSpec((tm, tn), lambda i, j: (i, j)),
        out_shape=jax.ShapeDtypeStruct((t, n), BF16),
        scratch_shapes=[pltpu.VMEM((tm, d), BF16)],
        compiler_params=_compiler_params(("parallel", "arbitrary"), vmem),
        name="norm_matmul",
    )(h, gain.reshape(1, d), w)


def _matmul_residual_kernel(*refs, n_in):
    a_refs, w_refs = refs[:n_in], refs[n_in:2 * n_in]
    r_ref, o_ref = refs[2 * n_in], refs[2 * n_in + 1]
    acc = r_ref[...]
    for a_ref, w_ref in zip(a_refs, w_refs):
        acc = acc + _dot(a_ref[...], w_ref[...])
    o_ref[...] = acc


def _matmul_residual(acts, w, res, tm=512):
    t, d = res.shape
    n_in = len(acts)
    k = acts[0].shape[1]
    assert all(a.shape[1] == k for a in acts) and w.shape == (n_in * k, d)
    vmem = 2 * 2 * tm * d * 4 + n_in * (2 * tm * k * 2 + 2 * k * d * 2)
    in_specs = [pl.BlockSpec((tm, k), lambda i: (i, 0)) for _ in acts]
    in_specs += [pl.BlockSpec((k, d), functools.partial(lambda i, idx: (idx, 0), idx=idx)) for idx in range(n_in)]
    weights = [w] * n_in
    in_specs.append(pl.BlockSpec((tm, d), lambda i: (i, 0)))
    return pl.pallas_call(
        functools.partial(_matmul_residual_kernel, n_in=n_in),
        grid=(t // tm,),
        in_specs=in_specs,
        out_specs=pl.BlockSpec((tm, d), lambda i: (i, 0)),
        out_shape=jax.ShapeDtypeStruct((t, d), F32),
        compiler_params=_compiler_params(("parallel",), vmem),
        name="matmul_residual",
    )(*acts, *weights, res)


def _gla_kernel(q_ref, k_ref, v_ref, r_ref, glr_ref, w2_ref, gb_ref, gn_ref, o_ref, st_ref, *, n_chunks):
    @pl.when(pl.program_id(1) == 0)
    def _():
        st_ref[...] = jnp.zeros_like(st_ref)

    c = GLA_CHUNK
    lb = n_chunks * c
    ri = lax.broadcasted_iota(jnp.int32, (lb, lb), 0)
    ci = lax.broadcasted_iota(jnp.int32, (lb, lb), 1)
    same_chunk_tril = (ri >= ci) & ((ri // c) == (ci // c))
    tril = jnp.where(same_chunk_tril, 1.0, 0.0).astype(BF16)
    causal = lax.broadcasted_iota(jnp.int32, (c, c), 0) >= lax.broadcasted_iota(jnp.int32, (c, c), 1)
    gn = gn_ref[...]

    z = _dot(glr_ref[...], w2_ref[...]) + gb_ref[...]
    log_a = (jnp.minimum(z, 0.0) - jnp.log1p(jnp.exp(-jnp.abs(z)))) * (1.0 / GLA_TAU)
    bcum = _select_dot_lhs(tril, log_a, 3)
    q_all = q_ref[...].astype(F32) * (GLA_DK ** -0.5) * jnp.exp(bcum)
    k_all = k_ref[...].astype(F32)
    k_in_all = k_all * jnp.exp(-bcum)

    items = [(h, idx) for h in range(GLA_HEADS) for idx in range(n_chunks)]
    kcols = lambda h: slice(h * GLA_DK, (h + 1) * GLA_DK)
    vcols = lambda h: slice(h * GLA_DV, (h + 1) * GLA_DV)
    rows = lambda idx: slice(idx * c, (idx + 1) * c)
    q_in, k_in, k_end, decay_end, v, v_t = {}, {}, {}, {}, {}, {}
    for h, idx in items:
        b_c = bcum[rows(idx), kcols(h)]
        b_last = b_c[c - 1:c, :]
        q_in[h, idx] = q_all[rows(idx), kcols(h)].astype(BF16)
        k_in[h, idx] = k_in_all[rows(idx), kcols(h)].astype(BF16)
        k_end[h, idx] = (k_all[rows(idx), kcols(h)] * jnp.exp(b_last - b_c)).astype(BF16)
        decay_end[h, idx] = jnp.exp(b_last)
        v[h, idx] = v_ref[rows(idx), vcols(h)]
        v_t[h, idx] = v[h, idx].astype(F32).T.astype(BF16)
    attn = {it: jnp.where(causal, _dot_nt(q_in[it], k_in[it]), 0.0).astype(BF16) for it in items}
    o_intra = {it: _dot(attn[it], v[it]) for it in items}
    upd = {it: _dot(v_t[it], k_end[it]) for it in items}
    for h in range(GLA_HEADS):
        state = st_ref[h]
        for idx in range(n_chunks):
            o = o_intra[h, idx] + _dot_nt(q_in[h, idx], state.astype(BF16))
            state = decay_end[h, idx] * state + upd[h, idx]
            o = _rms_norm(o, gn) * _silu(r_ref[rows(idx), vcols(h)].astype(F32))
            o_ref[rows(idx), vcols(h)] = o.astype(o_ref.dtype)
        st_ref[h] = state


def _gla(proj, w2p, gate_b, gla_norm, batch, seq, lb=256):
    t = batch * seq
    ns = seq // lb
    qk_w, v_w = GLA_HEADS * GLA_DK, GLA_HEADS * GLA_DV
    row = lambda b, n: b * ns + n
    vmem = 2 * lb * (2 * qk_w + 2 * v_w + LANES) * 2 + 2 * lb * v_w * 2 + 8 * lb * qk_w * 4 + (4 << 20)
    return pl.pallas_call(
        functools.partial(_gla_kernel, n_chunks=lb // GLA_CHUNK),
        grid=(batch, ns),
        in_specs=[
            pl.BlockSpec((lb, qk_w), lambda b, n: (row(b, n), HYB_Q // qk_w)),
            pl.BlockSpec((lb, qk_w), lambda b, n: (row(b, n), HYB_K // qk_w)),
            pl.BlockSpec((lb, v_w), lambda b, n: (row(b, n), HYB_V // v_w)),
            pl.BlockSpec((lb, v_w), lambda b, n: (row(b, n), HYB_R // v_w)),
            pl.BlockSpec((lb, LANES), lambda b, n: (row(b, n), HYB_GLR // LANES)),
            pl.BlockSpec((LANES, qk_w), lambda b, n: (0, 0)),
            pl.BlockSpec((1, qk_w), lambda b, n: (0, 0)),
            pl.BlockSpec((1, GLA_DV), lambda b, n: (0, 0)),
        ],
        out_specs=pl.BlockSpec((lb, v_w), lambda b, n: (row(b, n), 0)),
        out_shape=jax.ShapeDtypeStruct((t, v_w), BF16),
        scratch_shapes=[pltpu.VMEM((GLA_HEADS, GLA_DV, GLA_DK), F32)],
        compiler_params=_compiler_params(("parallel", "arbitrary"), vmem),
        name="gla",
    )(proj, proj, proj, proj, proj, w2p, gate_b.reshape(1, -1), gla_norm.reshape(1, -1))


def _swa_kernel(sink_ref, pos_ref, q_ref, kv_ref, qn_ref, kn_ref, freq_ref, bd_ref, o_ref, kprev_ref, vprev_ref):
    n = pl.program_id(1)
    w = SWA_BLOCK
    lane = lax.broadcasted_iota(jnp.int32, (w, LANES), 1)
    lane_lo = lane < SWA_HEAD_DIM
    lane_lo_kv = lax.broadcasted_iota(jnp.int32, (2 * w, LANES), 1) < SWA_HEAD_DIM
    dim = lane & (SWA_HEAD_DIM - 1)
    half = ROPE_DIM // 2
    bd = bd_ref[...]

    @pl.when(n == 0)
    def _():
        kprev_ref[...] = jnp.zeros_like(kprev_ref)
        vprev_ref[...] = jnp.zeros_like(vprev_ref)

    ang = pos_ref[...].astype(F32) * freq_ref[...]
    cos, sin = jnp.cos(ang), jnp.sin(ang)
    s_hi = jnp.where((dim >= half) & (dim < ROPE_DIM), sin, 0.0)
    s_lo = jnp.where(dim < half, -sin, 0.0)

    def norm_rope(x, gain):
        ms = _dot((x * x).astype(BF16), bd) * (1.0 / SWA_HEAD_DIM)
        xn = x * lax.rsqrt(ms + NORM_EPS) * gain
        return xn * cos + pltpu.roll(xn, half, 1) * s_hi + pltpu.roll(xn, LANES - half, 1) * s_lo

    k_cur = norm_rope(kv_ref[:, 0:LANES].astype(F32), kn_ref[...])
    v_cur = kv_ref[:, LANES:2 * LANES].astype(F32)
    kk = jnp.concatenate([kprev_ref[...], k_cur], axis=0)
    vv = jnp.concatenate([vprev_ref[...], v_cur], axis=0)
    kprev_ref[...] = k_cur
    vprev_ref[...] = v_cur
    kk_sw = pltpu.roll(kk, SWA_HEAD_DIM, 1)
    vv_sw = pltpu.roll(vv, SWA_HEAD_DIM, 1)
    k2 = [jnp.where(lane_lo_kv, kk, kk_sw).astype(BF16), jnp.where(lane_lo_kv, kk_sw, kk).astype(BF16)]
    v2 = [jnp.where(lane_lo_kv, vv, vv_sw).astype(BF16), jnp.where(lane_lo_kv, vv_sw, vv).astype(BF16)]

    qi = lax.broadcasted_iota(jnp.int32, (w, 2 * w), 0)
    ki = lax.broadcasted_iota(jnp.int32, (w, 2 * w), 1)
    rel = qi + w - ki
    valid = (rel >= 0) & (rel < SWA_WINDOW) & ((ki >= w) | (n > 0))

    qn = qn_ref[...] * (SWA_HEAD_DIM ** -0.5)
    heads_per_kv = SWA_Q_HEADS // SWA_KV_HEADS
    pairs = range(SWA_Q_HEADS // 2)
    heads = [(pair, hf) for pair in pairs for hf in range(2)]
    qp = [norm_rope(q_ref[:, pl.ds(pair * LANES, LANES)].astype(F32), qn) for pair in pairs]
    scores = []
    for pair, hf in heads:
        keep = lane_lo if hf == 0 else jnp.logical_not(lane_lo)
        qm = jnp.where(keep, qp[pair], 0.0).astype(BF16)
        scores.append(jnp.where(valid, _dot_nt(qm, k2[(2 * pair) // heads_per_kv]), -1e30))
    probs, rdenom = [], []
    for (pair, hf), s in zip(heads, scores):
        sink = sink_ref[2 * pair + hf]
        m = jnp.maximum(jnp.max(s, axis=-1, keepdims=True), sink)
        p = jnp.exp(s - m)
        rdenom.append(1.0 / (jnp.sum(p, axis=-1, keepdims=True) + jnp.exp(sink - m)))
        probs.append(p.astype(BF16))
    outs = [_dot(p, v2[(2 * pair) // heads_per_kv]) * r for (pair, hf), p, r in zip(heads, probs, rdenom)]
    for pair in pairs:
        o_ref[:, pl.ds(pair * LANES, LANES)] = jnp.where(lane_lo, outs[2 * pair], outs[2 * pair + 1]).astype(o_ref.dtype)


def _swa(proj, pos_col, sinks, q_norm, k_norm, batch, seq):
    t = batch * seq
    w = SWA_BLOCK
    nb = seq // w
    half = ROPE_DIM // 2
    inv_freq = ROPE_THETA ** (-2.0 * jnp.arange(half, dtype=F32) / ROPE_DIM)
    dim = np.arange(LANES) % SWA_HEAD_DIM
    freq_row = jnp.where(dim < ROPE_DIM, inv_freq[dim % half], 0.0).reshape(1, LANES).astype(F32)
    blockdiag = jnp.asarray(np.kron(np.eye(LANES // SWA_HEAD_DIM), np.ones((SWA_HEAD_DIM, SWA_HEAD_DIM))), BF16)
    cur = lambda b, n: b * nb + n
    q_w = SWA_Q_HEADS * SWA_HEAD_DIM
    vmem = 2 * w * (q_w + 2 * LANES) * 2 + 2 * w * q_w * 2 + 4 * w * LANES * 4 + (8 << 20)
    return pl.pallas_call(
        _swa_kernel,
        grid=(batch, nb),
        in_specs=[
            pl.BlockSpec(memory_space=pltpu.SMEM),
            pl.BlockSpec((w, 1), lambda b, n: (cur(b, n), 0)),
            pl.BlockSpec((w, q_w), lambda b, n: (cur(b, n), HYB_SQ // q_w)),
            pl.BlockSpec((w, 2 * LANES), lambda b, n: (cur(b, n), HYB_SK // (2 * LANES))),
            pl.BlockSpec((1, LANES), lambda b, n: (0, 0)),
            pl.BlockSpec((1, LANES), lambda b, n: (0, 0)),
            pl.BlockSpec((1, LANES), lambda b, n: (0, 0)),
            pl.BlockSpec((LANES, LANES), lambda b, n: (0, 0)),
        ],
        out_specs=pl.BlockSpec((w, q_w), lambda b, n: (cur(b, n), 0)),
        out_shape=jax.ShapeDtypeStruct((t, q_w), BF16),
        scratch_shapes=[pltpu.VMEM((w, LANES), F32), pltpu.VMEM((w, LANES), F32)],
        compiler_params=_compiler_params(("parallel", "arbitrary"), vmem),
        name="swa",
    )(sinks.astype(F32), pos_col, proj, proj,
      jnp.tile(q_norm, 2).reshape(1, LANES), jnp.tile(k_norm, 2).reshape(1, LANES), freq_row, blockdiag)


def _ssd_kernel(x_ref, b_ref, c_ref, z_ref, dt_ref, cw_ref, cb_ref, dtb_ref, alog_ref, dsk_ref, ng_ref, e_ref, bm_ref,
                o_ref, tail_ref, st_ref, *, n_chunks):
    n = pl.program_id(1)
    L = SSD_CHUNK
    lb = n_chunks * L
    gw = SSD_GROUP_W
    hp = SSD_HEAD_DIM
    ns = SSD_D_STATE
    groups = range(SSD_GROUPS)
    chunks = range(n_chunks)
    crow = lambda c: slice(c * L, (c + 1) * L)

    @pl.when(n == 0)
    def _():
        tail_ref[...] = jnp.zeros_like(tail_ref)
        st_ref[...] = jnp.zeros_like(st_ref)

    cur = jnp.concatenate([x_ref[...], b_ref[...], c_ref[...]], axis=1).astype(F32)
    ext = jnp.concatenate([tail_ref[...], cur], axis=0)
    tail_ref[...] = cur[lb - 8:lb, :]
    acc = cb_ref[...] + cw_ref[SSD_CONV - 1:SSD_CONV, :] * cur
    for j in range(SSD_CONV - 1):
        shifted = pltpu.roll(ext, SSD_CONV - 1 - j, 0)[8:8 + lb, :]
        acc = acc + cw_ref[j:j + 1, :] * shifted
    xbc = _silu(acc)

    lane = lax.broadcasted_iota(jnp.int32, (1, LANES), 1)
    dt = _softplus(dt_ref[...].astype(F32) + dtb_ref[...])
    a_neg = jnp.where(lane < SSD_HEADS, -jnp.exp(alog_ref[...]), 0.0)
    ri = lax.broadcasted_iota(jnp.int32, (lb, lb), 0)
    ci = lax.broadcasted_iota(jnp.int32, (lb, lb), 1)
    tril = jnp.where((ri >= ci) & ((ri // L) == (ci // L)), 1.0, 0.0).astype(BF16)
    acum = _select_dot_lhs(tril, dt * a_neg, 3)
    wgt_small = jnp.concatenate([jnp.exp(acum[c * L + L - 1:c * L + L, :] - acum[crow(c), :]) for c in chunks], axis=0)
    acum_cat = jnp.concatenate(_split_bf16(acum, 3), axis=1)
    dt_cat = jnp.concatenate(_split_bf16(dt, 2), axis=1)
    eacum_cat = jnp.concatenate(_split_bf16(jnp.exp(acum), 2), axis=1)
    wgt_cat = jnp.concatenate(_split_bf16(wgt_small, 2), axis=1)
    acum_t = [jnp.concatenate([acum[crow(c), :], acum[crow(c), :]], axis=0).T for c in chunks]

    lane_lo = lane < hp
    row_l = lax.broadcasted_iota(jnp.int32, (L, gw), 0)
    src_s = lax.broadcasted_iota(jnp.int32, (L, gw), 1) & (hp - 1)
    causal = src_s <= row_l
    blockmask = bm_ref[...]
    heads_per_group = SSD_HEADS // SSD_GROUPS

    xs, b_bf, c_bf = [], [], []
    for g in groups:
        xs.append(xbc[:, g * gw:(g + 1) * gw])
        b_bf.append(xbc[:, SSD_D_INNER + g * ns:SSD_D_INNER + (g + 1) * ns])
        c_bf.append(xbc[:, SSD_D_INNER + SSD_BC_W + g * ns:SSD_D_INNER + SSD_BC_W + (g + 1) * ns].astype(BF16))
    b_t = [[b_bf[g][crow(c), :].T.astype(BF16) for g in groups] for c in chunks]
    b_bf = [b.astype(BF16) for b in b_bf]

    col, dt_exp, ecol, wgt = [], [], [], []
    for g in groups:
        e_g = e_ref[:, pl.ds(g * gw, gw)]
        col.append(_dot(acum_cat, e_g))
        dt_exp.append(_dot(dt_cat, e_g[:2 * LANES, :]))
        ecol.append(_dot(eacum_cat, e_g[:2 * LANES, :]))
        wgt.append(_dot(wgt_cat, e_g[:2 * LANES, :]))
    cb = [[_dot_nt(c_bf[g][crow(c), :], b_bf[g][crow(c), :]) for g in groups] for c in chunks]

    xdt = [xs[g] * dt_exp[g] for g in groups]
    xdt_bf = [t.astype(BF16) for t in xdt]
    xw = [(xdt[g] * wgt[g]).astype(BF16) for g in groups]
    m = []
    for c in chunks:
        m_c = []
        for g in groups:
            row_parts = []
            for pr in range(heads_per_group // 2):
                h0 = g * heads_per_group + 2 * pr
                r = jnp.where(lane_lo, acum_t[c][h0:h0 + 1, :], acum_t[c][h0 + 1:h0 + 2, :])
                row_parts.append(jnp.broadcast_to(r, (L, LANES)))
            row = jnp.concatenate(row_parts, axis=1)
            decay = jnp.where(causal, jnp.exp(col[g][crow(c), :] - row), 0.0)
            cb2 = jnp.concatenate([cb[c][g], cb[c][g]], axis=1)
            cb_exp = jnp.concatenate([cb2] * (gw // LANES), axis=1)
            m_c.append((cb_exp * decay).astype(BF16))
        m.append(m_c)

    y_diag, upd = [], []
    for c in chunks:
        yd_c, upd_c = [], []
        for g in groups:
            y_parts = []
            for qd in range(gw // 256):
                sl = slice(qd * 256, (qd + 1) * 256)
                rhs = jnp.concatenate([xdt_bf[g][crow(c), sl]] * 4, axis=0) * blockmask
                y_parts.append(_dot(m[c][g][:, sl], rhs))
            yd_c.append(jnp.concatenate(y_parts, axis=1))
            upd_c.append(_dot(b_t[c][g], xw[g][crow(c), :]))
        y_diag.append(yd_c)
        upd.append(upd_c)

    for g in groups:
        cols = pl.ds(g * gw, gw)
        state = st_ref[g]
        y_g = []
        for c in chunks:
            y_off = _dot(c_bf[g][crow(c), :], state.astype(BF16)) * ecol[g][crow(c), :]
            y_g.append(y_diag[c][g] + y_off)
            state = ecol[g][c * L + L - 1:c * L + L, :] * state + upd[c][g]
        st_ref[g] = state
        y = jnp.concatenate(y_g, axis=0) + xs[g] * dsk_ref[:, cols]
        y = y * _silu(z_ref[:, cols].astype(F32))
        o_ref[:, cols] = _rms_norm(y, ng_ref[:, cols]).astype(o_ref.dtype)


def _ssd(proj, conv_w, conv_b, dt_bias, a_log, d_skip, norm_g, batch, seq, n_chunks=2):
    t = batch * seq
    L = n_chunks * SSD_CHUNK
    ns = seq // L
    pad = LANES - SSD_HEADS
    expand = np.kron(np.eye(LANES, SSD_HEADS), np.ones((1, SSD_HEAD_DIM)))
    expand = jnp.asarray(np.tile(expand, (3, 1)), BF16)
    blockmask = jnp.asarray(np.kron(np.eye(4), np.ones((SSD_HEAD_DIM, SSD_HEAD_DIM))), BF16)
    row = lambda b, n: b * ns + n
    const = lambda b, n: (0, 0)
    vmem = (2 * L * (SSD_CONV_CH + SSD_D_INNER + LANES) * 2 + 2 * L * SSD_D_INNER * 2
            + 2 * (6 * SSD_CONV_CH * 4 + 3 * LANES * SSD_D_INNER * 2)
            + 8 * SSD_CONV_CH * 4 + SSD_GROUPS * SSD_D_STATE * SSD_GROUP_W * 4 + (12 << 20))
    return pl.pallas_call(
        functools.partial(_ssd_kernel, n_chunks=n_chunks),
        grid=(batch, ns),
        in_specs=[
            pl.BlockSpec((L, SSD_D_INNER), lambda b, n: (row(b, n), SSD_X // SSD_D_INNER)),
            pl.BlockSpec((L, SSD_BC_W), lambda b, n: (row(b, n), SSD_B // SSD_BC_W)),
            pl.BlockSpec((L, SSD_BC_W), lambda b, n: (row(b, n), SSD_C // SSD_BC_W)),
            pl.BlockSpec((L, SSD_D_INNER), lambda b, n: (row(b, n), SSD_Z // SSD_D_INNER)),
            pl.BlockSpec((L, LANES), lambda b, n: (row(b, n), SSD_DT // LANES)),
            pl.BlockSpec((SSD_CONV, SSD_CONV_CH), const),
            pl.BlockSpec((1, SSD_CONV_CH), const),
            pl.BlockSpec((1, LANES), const),
            pl.BlockSpec((1, LANES), const),
            pl.BlockSpec((1, SSD_D_INNER), const),
            pl.BlockSpec((1, SSD_D_INNER), const),
            pl.BlockSpec((3 * LANES, SSD_D_INNER), const),
            pl.BlockSpec((256, 256), const),
        ],
        out_specs=pl.BlockSpec((L, SSD_D_INNER), lambda b, n: (row(b, n), 0)),
        out_shape=jax.ShapeDtypeStruct((t, SSD_D_INNER), BF16),
        scratch_shapes=[pltpu.VMEM((8, SSD_CONV_CH), F32),
                        pltpu.VMEM((SSD_GROUPS, SSD_D_STATE, SSD_GROUP_W), F32)],
        compiler_params=_compiler_params(("parallel", "arbitrary"), vmem),
        name="ssd",
    )(proj, proj, proj, proj, proj, conv_w, conv_b.reshape(1, -1),
      jnp.pad(dt_bias, (0, pad)).reshape(1, LANES), jnp.pad(a_log, (0, pad)).reshape(1, LANES),
      jnp.repeat(d_skip, SSD_HEAD_DIM).reshape(1, -1), norm_g.reshape(1, -1), expand, blockmask)


def _hyb_weight(w_in):
    w = w_in.astype(BF16)
    glr_end = HYB_SQ + GLA_GATE_RANK
    zeros = jnp.zeros((w.shape[0], HYB_W - HYB_GLR - GLA_GATE_RANK), BF16)
    return jnp.concatenate([w[:, :HYB_SQ], w[:, glr_end:], w[:, HYB_SQ:glr_end], zeros], axis=1)


def _ssd_weight(w_in):
    return jnp.pad(w_in, ((0, 0), (0, SSD_W - w_in.shape[1]))).astype(BF16)


def _gla_swa_mixer(h, pos_col, norm_gain, w_in, gate_w2, gate_b, gla_norm, q_norm, k_norm, sinks, w_out, batch, seq):
    proj = _norm_matmul(h, norm_gain, _hyb_weight(w_in), tn=1536)
    w2p = jnp.pad(gate_w2, ((0, LANES - GLA_GATE_RANK), (0, 0))).astype(BF16)
    o_gla = _gla(proj, w2p, gate_b, gla_norm, batch, seq)
    o_swa = _swa(proj, pos_col, sinks, q_norm, k_norm, batch, seq)
    return _matmul_residual([o_gla, o_swa], w_out.astype(BF16), h)


def _ssd_mixer(h, norm_gain, w_in, conv_w, conv_b, dt_bias, a_log, d_skip, norm_g, w_out, batch, seq):
    proj = _norm_matmul(h, norm_gain, _ssd_weight(w_in), tn=1536)
    y = _ssd(proj, conv_w, conv_b, dt_bias, a_log, d_skip, norm_g, batch, seq)
    return _matmul_residual([y], w_out.astype(BF16), h)


def kernel(x, positions, norm_ffn, w_ffn_gu, w_ffn_down, norm_mix, hyb_w_in, gla_gate_w2, gla_gate_b, gla_norm,
           attn_q_norm, attn_k_norm, attn_sinks, hyb_w_out, ssd_w_in, ssd_conv_w, ssd_conv_b, ssd_dt_bias, ssd_a_log,
           ssd_d, ssd_norm, ssd_w_out):
    batch, seq, d = x.shape
    h = x.reshape(batch * seq, d)
    pos_col = positions.reshape(batch * seq, 1)
    depth = norm_ffn.shape[0]
    ffns = [(layer, pos) for layer in range(depth) for pos in range(2)]
    weights = (w_ffn_gu[0, 0].astype(BF16), w_ffn_down[0, 0].astype(BF16))
    for layer in range(depth):
        i = layer // 2
        for pos in range(2):
            if pos == 1:
                if layer % 2 == 0:
                    h = _gla_swa_mixer(h, pos_col, norm_mix[layer], hyb_w_in[i], gla_gate_w2[i], gla_gate_b[i],
                                       gla_norm[i], attn_q_norm[i], attn_k_norm[i], attn_sinks[i], hyb_w_out[i],
                                       batch, seq)
                else:
                    h = _ssd_mixer(h, norm_mix[layer], ssd_w_in[i], ssd_conv_w[i], ssd_conv_b[i], ssd_dt_bias[i],
                                   ssd_a_log[i], ssd_d[i], ssd_norm[i], ssd_w_out[i], batch, seq)
            k = ffns.index((layer, pos))
            nxt = (w_ffn_gu, w_ffn_down) + ffns[k + 1] if k + 1 < len(ffns) else None
            h, weights = _ffn(h, norm_ffn[layer, pos], weights[0], weights[1], nxt)
    return h.reshape(batch, seq, d)
```

```python
import functools

import numpy as np
import jax
import jax.numpy as jnp
from jax import lax
from jax.experimental import pallas as pl
from jax.experimental.pallas import tpu as pltpu

F32 = jnp.float32
BF16 = jnp.bfloat16

D_MODEL = 2048
D_FF = 5632
NORM_EPS = 1e-6
MACARON_WEIGHT = 0.5

GLA_HEADS = 4
GLA_DK = 128
GLA_DV = 256
GLA_GATE_RANK = 16
GLA_TAU = 16.0
GLA_CHUNK = 64

SWA_HEAD_DIM = 64
SWA_Q_HEADS = 16
SWA_KV_HEADS = 2
SWA_WINDOW = 128
SWA_BLOCK = 128
ROPE_THETA = 500000.0
ROPE_DIM = 16

SSD_D_INNER = 4096
SSD_HEAD_DIM = 64
SSD_HEADS = 64
SSD_GROUPS = 8
SSD_D_STATE = 128
SSD_CONV = 4
SSD_CHUNK = 64
SSD_GROUP_W = SSD_D_INNER // SSD_GROUPS
SSD_BC_W = SSD_GROUPS * SSD_D_STATE
SSD_CONV_CH = SSD_D_INNER + 2 * SSD_BC_W

LANES = 128
VMEM_LIMIT_CAP = 60 * 1024 * 1024

HYB_Q, HYB_K, HYB_V, HYB_R, HYB_SQ, HYB_SK, HYB_SV, HYB_GLR = 0, 512, 1024, 2048, 3072, 4096, 4224, 4352
HYB_W = 4608
SSD_Z, SSD_X, SSD_B, SSD_C, SSD_DT = 0, 4096, 8192, 9216, 10240
SSD_W = 10752


def _compiler_params(semantics, vmem_bytes):
    limit = min(int(vmem_bytes * 1.25) + (4 << 20), VMEM_LIMIT_CAP)
    return pltpu.CompilerParams(dimension_semantics=semantics, vmem_limit_bytes=limit)


def _rms_norm(x, gain):
    ms = jnp.mean(x * x, axis=-1, keepdims=True)
    return x * lax.rsqrt(ms + NORM_EPS) * gain


def _silu(x):
    return x * jax.nn.sigmoid(x)


def _softplus(x):
    return jnp.maximum(x, 0.0) + jnp.log1p(jnp.exp(-jnp.abs(x)))


def _split_bf16(x, n):
    parts, rest = [], x
    for _ in range(n):
        p = rest.astype(BF16)
        parts.append(p)
        rest = rest - p.astype(F32)
    return parts


def _dot(a, b):
    return jnp.dot(a, b, preferred_element_type=F32)


def _dot_nt(a, b):
    return lax.dot_general(a, b, (((1,), (1,)), ((), ())), preferred_element_type=F32)


def _select_dot_lhs(sel, x, n):
    return sum(_dot(sel, p) for p in _split_bf16(x, n))


def _select_dot_rhs(x, sel, n):
    return sum(_dot(p, sel) for p in _split_bf16(x, n))


def _ffn_kernel(x_ref, g_ref, wg_ref, wu_ref, wd_ref, *rest, cast_next):
    if cast_next:
        ngu_ref, nd_ref, o_ref, ogu_ref, od_ref, hn_ref = rest
    else:
        o_ref, hn_ref = rest

    @pl.when(pl.program_id(1) == 0)
    def _():
        x = x_ref[...]
        hn_ref[...] = _rms_norm(x, g_ref[...]).astype(BF16)
        o_ref[...] = x

    h = hn_ref[...]
    gate = _dot(h, wg_ref[...])
    up = _dot(h, wu_ref[...])
    act = (_silu(gate) * (up * MACARON_WEIGHT)).astype(BF16)
    o_ref[...] += _dot(act, wd_ref[...])
    if cast_next:
        ogu_ref[...] = ngu_ref[...].astype(BF16)
        od_ref[...] = nd_ref[...].astype(BF16)


def _ffn(h, gain, w_gu, w_down, next_weights=None, tm=1024, tf=512):
    t, d = h.shape
    nf = D_FF // tf
    ni = t // tm
    vmem = 2 * (2 * tm * d * 4) + 2 * 3 * d * tf * 2 + tm * d * 2 + 3 * tm * tf * 4
    in_specs = [
        pl.BlockSpec((tm, d), lambda i, j: (i, 0)),
        pl.BlockSpec((1, d), lambda i, j: (0, 0)),
        pl.BlockSpec((d, tf), lambda i, j: (0, j)),
        pl.BlockSpec((d, tf), lambda i, j: (0, j + nf)),
        pl.BlockSpec((tf, d), lambda i, j: (j, 0)),
    ]
    out_specs = [pl.BlockSpec((tm, d), lambda i, j: (i, 0))]
    out_shape = [jax.ShapeDtypeStruct((t, d), F32)]
    args = [h, gain.reshape(1, d), w_gu, w_gu, w_down]
    if next_weights is not None:
        w_gu_all, w_down_all, layer, pos = next_weights
        gu_blk = (d // ni, 2 * D_FF // nf)
        dn_blk = (D_FF // nf, d // ni)
        in_specs += [
            pl.BlockSpec((None, None) + gu_blk, lambda i, j: (layer, pos, i, j)),
            pl.BlockSpec((None, None) + dn_blk, lambda i, j: (layer, pos, j, i)),
        ]
        out_specs += [pl.BlockSpec(gu_blk, lambda i, j: (i, j)), pl.BlockSpec(dn_blk, lambda i, j: (j, i))]
        out_shape += [jax.ShapeDtypeStruct((d, 2 * D_FF), BF16), jax.ShapeDtypeStruct((D_FF, d), BF16)]
        args += [w_gu_all, w_down_all]
        vmem += 2 * (gu_blk[0] * gu_blk[1] + dn_blk[0] * dn_blk[1]) * (4 + 2)
    outs = pl.pallas_call(
        functools.partial(_ffn_kernel, cast_next=next_weights is not None),
        grid=(ni, nf),
        in_specs=in_specs,
        out_specs=out_specs,
        out_shape=out_shape,
        scratch_shapes=[pltpu.VMEM((tm, d), BF16)],
        compiler_params=_compiler_params(("parallel", "arbitrary"), vmem),
        name="ffn",
    )(*args)
    return outs[0], (tuple(outs[1:]) if next_weights is not None else None)


def _norm_matmul_kernel(x_ref, g_ref, w_ref, o_ref, hn_ref):
    @pl.when(pl.program_id(1) == 0)
    def _():
        hn_ref[...] = _rms_norm(x_ref[...], g_ref[...]).astype(BF16)

    o_ref[...] = _dot(hn_ref[...], w_ref[...]).astype(o_ref.dtype)


def _norm_matmul(h, gain, w, tn, tm=1024):
    t, d = h.shape
    n = w.shape[1]
    vmem = 2 * tm * d * 4 + 2 * d * tn * 2 + 2 * tm * tn * 2 + tm * d * 2 + tm * tn * 4
    return pl.pallas_call(
        _norm_matmul_kernel,
        grid=(t // tm, n // tn),
        in_specs=[
            pl.BlockSpec((tm, d), lambda i, j: (i, 0)),
            pl.BlockSpec((1, d), lambda i, j: (0, 0)),
            pl.BlockSpec((d, tn), lambda i, j: (0, j)),
        ],
        out_specs=pl.BlockSpec((tm, tn), lambda i, j: (i, j)),
        out_shape=jax.ShapeDtypeStruct((t, n), BF16),
        scratch_shapes=[pltpu.VMEM((tm, d), BF16)],
        compiler_params=_compiler_params(("parallel", "arbitrary"), vmem),
        name="norm_matmul",
    )(h, gain.reshape(1, d), w)


def _matmul_residual_kernel(*refs, n_in):
    a_refs, w_refs = refs[:n_in], refs[n_in:2 * n_in]
    r_ref, o_ref = refs[2 * n_in], refs[2 * n_in + 1]
    acc = r_ref[...]
    for a_ref, w_ref in zip(a_refs, w_refs):
        acc = acc + _dot(a_ref[...], w_ref[...])
    o_ref[...] = acc


def _matmul_residual(acts, w, res, tm=512):
    t, d = res.shape
    n_in = len(acts)
    k = acts[0].shape[1]
    assert all(a.shape[1] == k for a in acts) and w.shape == (n_in * k, d)
    vmem = 2 * 2 * tm * d * 4 + n_in * (2 * tm * k * 2 + 2 * k * d * 2)
    in_specs = [pl.BlockSpec((tm, k), lambda i: (i, 0)) for _ in acts]
    in_specs += [pl.BlockSpec((k, d), functools.partial(lambda i, idx: (idx, 0), idx=idx)) for idx in range(n_in)]
    weights = [w] * n_in
    in_specs.append(pl.BlockSpec((tm, d), lambda i: (i, 0)))
    return pl.pallas_call(
        functools.partial(_matmul_residual_kernel, n_in=n_in),
        grid=(t // tm,),
        in_specs=in_specs,
        out_specs=pl.BlockSpec((tm, d), lambda i: (i, 0)),
        out_shape=jax.ShapeDtypeStruct((t, d), F32),
        compiler_params=_compiler_params(("parallel",), vmem),
        name="matmul_residual",
    )(*acts, *weights, res)


def _gla_kernel(q_ref, k_ref, v_ref, r_ref, glr_ref, w2_ref, gb_ref, gn_ref, o_ref, st_ref, *, n_chunks):
    @pl.when(pl.program_id(1) == 0)
    def _():
        st_ref[...] = jnp.zeros_like(st_ref)

    c = GLA_CHUNK
    lb = n_chunks * c
    ri = lax.broadcasted_iota(jnp.int32, (lb, lb), 0)
    ci = lax.broadcasted_iota(jnp.int32, (lb, lb), 1)
    same_chunk_tril = (ri >= ci) & ((ri // c) == (ci // c))
    tril = jnp.where(same_chunk_tril, 1.0, 0.0).astype(BF16)
    causal = lax.broadcasted_iota(jnp.int32, (c, c), 0) >= lax.broadcasted_iota(jnp.int32, (c, c), 1)
    gn = gn_ref[...]

    z = _dot(glr_ref[...], w2_ref[...]) + gb_ref[...]
    log_a = (jnp.minimum(z, 0.0) - jnp.log1p(jnp.exp(-jnp.abs(z)))) * (1.0 / GLA_TAU)
    bcum = _select_dot_lhs(tril, log_a, 3)
    q_all = q_ref[...].astype(F32) * (GLA_DK ** -0.5) * jnp.exp(bcum)
    k_all = k_ref[...].astype(F32)
    k_in_all = k_all * jnp.exp(-bcum)

    items = [(h, idx) for h in range(GLA_HEADS) for idx in range(n_chunks)]
    kcols = lambda h: slice(h * GLA_DK, (h + 1) * GLA_DK)
    vcols = lambda h: slice(h * GLA_DV, (h + 1) * GLA_DV)
    rows = lambda idx: slice(idx * c, (idx + 1) * c)
    q_in, k_in, k_end, decay_end, v, v_t = {}, {}, {}, {}, {}, {}
    for h, idx in items:
        b_c = bcum[rows(idx), kcols(h)]
        b_last = b_c[c - 1:c, :]
        q_in[h, idx] = q_all[rows(idx), kcols(h)].astype(BF16)
        k_in[h, idx] = k_in_all[rows(idx), kcols(h)].astype(BF16)
        k_end[h, idx] = (k_all[rows(idx), kcols(h)] * jnp.exp(b_last - b_c)).astype(BF16)
        decay_end[h, idx] = jnp.exp(b_last)
        v[h, idx] = v_ref[rows(idx), vcols(h)]
        v_t[h, idx] = v[h, idx].astype(F32).T.astype(BF16)
    attn = {it: jnp.where(causal, _dot_nt(q_in[it], k_in[it]), 0.0).astype(BF16) for it in items}
    o_intra = {it: _dot(attn[it], v[it]) for it in items}
    upd = {it: _dot(v_t[it], k_end[it]) for it in items}
    for h in range(GLA_HEADS):
        state = st_ref[h]
        for idx in range(n_chunks):
            o = o_intra[h, idx] + _dot_nt(q_in[h, idx], state.astype(BF16))
            state = decay_end[h, idx] * state + upd[h, idx]
            o = _rms_norm(o, gn) * _silu(r_ref[rows(idx), vcols(h)].astype(F32))
            o_ref[rows(idx), vcols(h)] = o.astype(o_ref.dtype)
        st_ref[h] = state


def _gla(proj, w2p, gate_b, gla_norm, batch, seq, lb=256):
    t = batch * seq
    ns = seq // lb
    qk_w, v_w = GLA_HEADS * GLA_DK, GLA_HEADS * GLA_DV
    row = lambda b, n: b * ns + n
    vmem = 2 * lb * (2 * qk_w + 2 * v_w + LANES) * 2 + 2 * lb * v_w * 2 + 8 * lb * qk_w * 4 + (4 << 20)
    return pl.pallas_call(
        functools.partial(_gla_kernel, n_chunks=lb // GLA_CHUNK),
        grid=(batch, ns),
        in_specs=[
            pl.BlockSpec((lb, qk_w), lambda b, n: (row(b, n), HYB_Q // qk_w)),
            pl.BlockSpec((lb, qk_w), lambda b, n: (row(b, n), HYB_K // qk_w)),
            pl.BlockSpec((lb, v_w), lambda b, n: (row(b, n), HYB_V // v_w)),
            pl.BlockSpec((lb, v_w), lambda b, n: (row(b, n), HYB_R // v_w)),
            pl.BlockSpec((lb, LANES), lambda b, n: (row(b, n), HYB_GLR // LANES)),
            pl.BlockSpec((LANES, qk_w), lambda b, n: (0, 0)),
            pl.BlockSpec((1, qk_w), lambda b, n: (0, 0)),
            pl.BlockSpec((1, GLA_DV), lambda b, n: (0, 0)),
        ],
        out_specs=pl.BlockSpec((lb, v_w), lambda b, n: (row(b, n), 0)),
        out_shape=jax.ShapeDtypeStruct((t, v_w), BF16),
        scratch_shapes=[pltpu.VMEM((GLA_HEADS, GLA_DV, GLA_DK), F32)],
        compiler_params=_compiler_params(("parallel", "arbitrary"), vmem),
        name="gla",
    )(proj, proj, proj, proj, proj, w2p, gate_b.reshape(1, -1), gla_norm.reshape(1, -1))


def _swa_kernel(sink_ref, pos_ref, q_ref, kv_ref, qn_ref, kn_ref, freq_ref, bd_ref, o_ref, kprev_ref, vprev_ref):
    n = pl.program_id(1)
    w = SWA_BLOCK
    lane = lax.broadcasted_iota(jnp.int32, (w, LANES), 1)
    lane_lo = lane < SWA_HEAD_DIM
    lane_lo_kv = lax.broadcasted_iota(jnp.int32, (2 * w, LANES), 1) < SWA_HEAD_DIM
    dim = lane & (SWA_HEAD_DIM - 1)
    half = ROPE_DIM // 2
    bd = bd_ref[...]

    @pl.when(n == 0)
    def _():
        kprev_ref[...] = jnp.zeros_like(kprev_ref)
        vprev_ref[...] = jnp.zeros_like(vprev_ref)

    ang = pos_ref[...].astype(F32) * freq_ref[...]
    cos, sin = jnp.cos(ang), jnp.sin(ang)
    s_hi = jnp.where((dim >= half) & (dim < ROPE_DIM), sin, 0.0)
    s_lo = jnp.where(dim < half, -sin, 0.0)

    def norm_rope(x, gain):
        ms = _dot((x * x).astype(BF16), bd) * (1.0 / SWA_HEAD_DIM)
        xn = x * lax.rsqrt(ms + NORM_EPS) * gain
        return xn * cos + pltpu.roll(xn, half, 1) * s_hi + pltpu.roll(xn, LANES - half, 1) * s_lo

    k_cur = norm_rope(kv_ref[:, 0:LANES].astype(F32), kn_ref[...])
    v_cur = kv_ref[:, LANES:2 * LANES].astype(F32)
    kk = jnp.concatenate([kprev_ref[...], k_cur], axis=0)
    vv = jnp.concatenate([vprev_ref[...], v_cur], axis=0)
    kprev_ref[...] = k_cur
    vprev_ref[...] = v_cur
    kk_sw = pltpu.roll(kk, SWA_HEAD_DIM, 1)
    vv_sw = pltpu.roll(vv, SWA_HEAD_DIM, 1)
    k2 = [jnp.where(lane_lo_kv, kk, kk_sw).astype(BF16), jnp.where(lane_lo_kv, kk_sw, kk).astype(BF16)]
    v2 = [jnp.where(lane_lo_kv, vv, vv_sw).astype(BF16), jnp.where(lane_lo_kv, vv_sw, vv).astype(BF16)]

    qi = lax.broadcasted_iota(jnp.int32, (w, 2 * w), 0)
    ki = lax.broadcasted_iota(jnp.int32, (w, 2 * w), 1)
    rel = qi + w - ki
    valid = (rel >= 0) & (rel < SWA_WINDOW) & ((ki >= w) | (n > 0))

    qn = qn_ref[...] * (SWA_HEAD_DIM ** -0.5)
    heads_per_kv = SWA_Q_HEADS // SWA_KV_HEADS
    pairs = range(SWA_Q_HEADS // 2)
    heads = [(pair, hf) for pair in pairs for hf in range(2)]
    qp = [norm_rope(q_ref[:, pl.ds(pair * LANES, LANES)].astype(F32), qn) for pair in pairs]
    scores = []
    for pair, hf in heads:
        keep = lane_lo if hf == 0 else jnp.logical_not(lane_lo)
        qm = jnp.where(keep, qp[pair], 0.0).astype(BF16)
        scores.append(jnp.where(valid, _dot_nt(qm, k2[(2 * pair) // heads_per_kv]), -1e30))
    probs, rdenom = [], []
    for (pair, hf), s in zip(heads, scores):
        sink = sink_ref[2 * pair + hf]
        m = jnp.maximum(jnp.max(s, axis=-1, keepdims=True), sink)
        p = jnp.exp(s - m)
        rdenom.append(1.0 / (jnp.sum(p, axis=-1, keepdims=True) + jnp.exp(sink - m)))
        probs.append(p.astype(BF16))
    outs = [_dot(p, v2[(2 * pair) // heads_per_kv]) * r for (pair, hf), p, r in zip(heads, probs, rdenom)]
    for pair in pairs:
        o_ref[:, pl.ds(pair * LANES, LANES)] = jnp.where(lane_lo, outs[2 * pair], outs[2 * pair + 1]).astype(o_ref.dtype)


def _swa(proj, pos_col, sinks, q_norm, k_norm, batch, seq):
    t = batch * seq
    w = SWA_BLOCK
    nb = seq // w
    half = ROPE_DIM // 2
    inv_freq = ROPE_THETA ** (-2.0 * jnp.arange(half, dtype=F32) / ROPE_DIM)
    dim = np.arange(LANES) % SWA_HEAD_DIM
    freq_row = jnp.where(dim < ROPE_DIM, inv_freq[dim % half], 0.0).reshape(1, LANES).astype(F32)
    blockdiag = jnp.asarray(np.kron(np.eye(LANES // SWA_HEAD_DIM), np.ones((SWA_HEAD_DIM, SWA_HEAD_DIM))), BF16)
    cur = lambda b, n: b * nb + n
    q_w = SWA_Q_HEADS * SWA_HEAD_DIM
    vmem = 2 * w * (q_w + 2 * LANES) * 2 + 2 * w * q_w * 2 + 4 * w * LANES * 4 + (8 << 20)
    return pl.pallas_call(
        _swa_kernel,
        grid=(batch, nb),
        in_specs=[
            pl.BlockSpec(memory_space=pltpu.SMEM),
            pl.BlockSpec((w, 1), lambda b, n: (cur(b, n), 0)),
            pl.BlockSpec((w, q_w), lambda b, n: (cur(b, n), HYB_SQ // q_w)),
            pl.BlockSpec((w, 2 * LANES), lambda b, n: (cur(b, n), HYB_SK // (2 * LANES))),
            pl.BlockSpec((1, LANES), lambda b, n: (0, 0)),
            pl.BlockSpec((1, LANES), lambda b, n: (0, 0)),
            pl.BlockSpec((1, LANES), lambda b, n: (0, 0)),
            pl.BlockSpec((LANES, LANES), lambda b, n: (0, 0)),
        ],
        out_specs=pl.BlockSpec((w, q_w), lambda b, n: (cur(b, n), 0)),
        out_shape=jax.ShapeDtypeStruct((t, q_w), BF16),
        scratch_shapes=[pltpu.VMEM((w, LANES), F32), pltpu.VMEM((w, LANES), F32)],
        compiler_params=_compiler_params(("parallel", "arbitrary"), vmem),
        name="swa",
    )(sinks.astype(F32), pos_col, proj, proj,
      jnp.tile(q_norm, 2).reshape(1, LANES), jnp.tile(k_norm, 2).reshape(1, LANES), freq_row, blockdiag)


def _ssd_kernel(x_ref, b_ref, c_ref, z_ref, dt_ref, cw_ref, cb_ref, dtb_ref, alog_ref, dsk_ref, ng_ref, e_ref, bm_ref,
                o_ref, tail_ref, st_ref, *, n_chunks):
    n = pl.program_id(1)
    L = SSD_CHUNK
    lb = n_chunks * L
    gw = SSD_GROUP_W
    hp = SSD_HEAD_DIM
    ns = SSD_D_STATE
    groups = range(SSD_GROUPS)
    chunks = range(n_chunks)
    crow = lambda c: slice(c * L, (c + 1) * L)

    @pl.when(n == 0)
    def _():
        tail_ref[...] = jnp.zeros_like(tail_ref)
        st_ref[...] = jnp.zeros_like(st_ref)

    cur = jnp.concatenate([x_ref[...], b_ref[...], c_ref[...]], axis=1).astype(F32)
    ext = jnp.concatenate([tail_ref[...], cur], axis=0)
    tail_ref[...] = cur[lb - 8:lb, :]
    acc = cb_ref[...] + cw_ref[SSD_CONV - 1:SSD_CONV, :] * cur
    for j in range(SSD_CONV - 1):
        shifted = pltpu.roll(ext, SSD_CONV - 1 - j, 0)[8:8 + lb, :]
        acc = acc + cw_ref[j:j + 1, :] * shifted
    xbc = _silu(acc)

    lane = lax.broadcasted_iota(jnp.int32, (1, LANES), 1)
    dt = _softplus(dt_ref[...].astype(F32) + dtb_ref[...])
    a_neg = jnp.where(lane < SSD_HEADS, -jnp.exp(alog_ref[...]), 0.0)
    ri = lax.broadcasted_iota(jnp.int32, (lb, lb), 0)
    ci = lax.broadcasted_iota(jnp.int32, (lb, lb), 1)
    tril = jnp.where((ri >= ci) & ((ri // L) == (ci // L)), 1.0, 0.0).astype(BF16)
    acum = _select_dot_lhs(tril, dt * a_neg, 3)
    wgt_small = jnp.concatenate([jnp.exp(acum[c * L + L - 1:c * L + L, :] - acum[crow(c), :]) for c in chunks], axis=0)
    acum_cat = jnp.concatenate(_split_bf16(acum, 3), axis=1)
    dt_cat = jnp.concatenate(_split_bf16(dt, 2), axis=1)
    eacum_cat = jnp.concatenate(_split_bf16(jnp.exp(acum), 2), axis=1)
    wgt_cat = jnp.concatenate(_split_bf16(wgt_small, 2), axis=1)
    acum_t = [jnp.concatenate([acum[crow(c), :], acum[crow(c), :]], axis=0).T for c in chunks]

    lane_lo = lane < hp
    row_l = lax.broadcasted_iota(jnp.int32, (L, gw), 0)
    src_s = lax.broadcasted_iota(jnp.int32, (L, gw), 1) & (hp - 1)
    causal = src_s <= row_l
    blockmask = bm_ref[...]
    heads_per_group = SSD_HEADS // SSD_GROUPS

    xs, b_bf, c_bf = [], [], []
    for g in groups:
        xs.append(xbc[:, g * gw:(g + 1) * gw])
        b_bf.append(xbc[:, SSD_D_INNER + g * ns:SSD_D_INNER + (g + 1) * ns])
        c_bf.append(xbc[:, SSD_D_INNER + SSD_BC_W + g * ns:SSD_D_INNER + SSD_BC_W + (g + 1) * ns].astype(BF16))
    b_t = [[b_bf[g][crow(c), :].T.astype(BF16) for g in groups] for c in chunks]
    b_bf = [b.astype(BF16) for b in b_bf]

    col, dt_exp, ecol, wgt = [], [], [], []
    for g in groups:
        e_g = e_ref[:, pl.ds(g * gw, gw)]
        col.append(_dot(acum_cat, e_g))
        dt_exp.append(_dot(dt_cat, e_g[:2 * LANES, :]))
        ecol.append(_dot(eacum_cat, e_g[:2 * LANES, :]))
        wgt.append(_dot(wgt_cat, e_g[:2 * LANES, :]))
    cb = [[_dot_nt(c_bf[g][crow(c), :], b_bf[g][crow(c), :]) for g in groups] for c in chunks]

    xdt = [xs[g] * dt_exp[g] for g in groups]
    xdt_bf = [t.astype(BF16) for t in xdt]
    xw = [(xdt[g] * wgt[g]).astype(BF16) for g in groups]
    m = []
    for c in chunks:
        m_c = []
        for g in groups:
            row_parts = []
            for pr in range(heads_per_group // 2):
                h0 = g * heads_per_group + 2 * pr
                r = jnp.where(lane_lo, acum_t[c][h0:h0 + 1, :], acum_t[c][h0 + 1:h0 + 2, :])
                row_parts.append(jnp.broadcast_to(r, (L, LANES)))
            row = jnp.concatenate(row_parts, axis=1)
            decay = jnp.where(causal, jnp.exp(col[g][crow(c), :] - row), 0.0)
            cb2 = jnp.concatenate([cb[c][g], cb[c][g]], axis=1)
            cb_exp = jnp.concatenate([cb2] * (gw // LANES), axis=1)
            m_c.append((cb_exp * decay).astype(BF16))
        m.append(m_c)

    y_diag, upd = [], []
    for c in chunks:
        yd_c, upd_c = [], []
        for g in groups:
            y_parts = []
            for qd in range(gw // 256):
                sl = slice(qd * 256, (qd + 1) * 256)
                rhs = jnp.concatenate([xdt_bf[g][crow(c), sl]] * 4, axis=0) * blockmask
                y_parts.append(_dot(m[c][g][:, sl], rhs))
            yd_c.append(jnp.concatenate(y_parts, axis=1))
            upd_c.append(_dot(b_t[c][g], xw[g][crow(c), :]))
        y_diag.append(yd_c)
        upd.append(upd_c)

    for g in groups:
        cols = pl.ds(g * gw, gw)
        state = st_ref[g]
        y_g = []
        for c in chunks:
            y_off = _dot(c_bf[g][crow(c), :], state.astype(BF16)) * ecol[g][crow(c), :]
            y_g.append(y_diag[c][g] + y_off)
            state = ecol[g][c * L + L - 1:c * L + L, :] * state + upd[c][g]
        st_ref[g] = state
        y = jnp.concatenate(y_g, axis=0) + xs[g] * dsk_ref[:, cols]
        y = y * _silu(z_ref[:, cols].astype(F32))
        o_ref[:, cols] = _rms_norm(y, ng_ref[:, cols]).astype(o_ref.dtype)


def _ssd(proj, conv_w, conv_b, dt_bias, a_log, d_skip, norm_g, batch, seq, n_chunks=2):
    t = batch * seq
    L = n_chunks * SSD_CHUNK
    ns = seq // L
    pad = LANES - SSD_HEADS
    expand = np.kron(np.eye(LANES, SSD_HEADS), np.ones((1, SSD_HEAD_DIM)))
    expand = jnp.asarray(np.tile(expand, (3, 1)), BF16)
    blockmask = jnp.asarray(np.kron(np.eye(4), np.ones((SSD_HEAD_DIM, SSD_HEAD_DIM))), BF16)
    row = lambda b, n: b * ns + n
    const = lambda b, n: (0, 0)
    vmem = (2 * L * (SSD_CONV_CH + SSD_D_INNER + LANES) * 2 + 2 * L * SSD_D_INNER * 2
            + 2 * (6 * SSD_CONV_CH * 4 + 3 * LANES * SSD_D_INNER * 2)
            + 8 * SSD_CONV_CH * 4 + SSD_GROUPS * SSD_D_STATE * SSD_GROUP_W * 4 + (12 << 20))
    return pl.pallas_call(
        functools.partial(_ssd_kernel, n_chunks=n_chunks),
        grid=(batch, ns),
        in_specs=[
            pl.BlockSpec((L, SSD_D_INNER), lambda b, n: (row(b, n), SSD_X // SSD_D_INNER)),
            pl.BlockSpec((L, SSD_BC_W), lambda b, n: (row(b, n), SSD_B // SSD_BC_W)),
            pl.BlockSpec((L, SSD_BC_W), lambda b, n: (row(b, n), SSD_C // SSD_BC_W)),
            pl.BlockSpec((L, SSD_D_INNER), lambda b, n: (row(b, n), SSD_Z // SSD_D_INNER)),
            pl.BlockSpec((L, LANES), lambda b, n: (row(b, n), SSD_DT // LANES)),
            pl.BlockSpec((SSD_CONV, SSD_CONV_CH), const),
            pl.BlockSpec((1, SSD_CONV_CH), const),
            pl.BlockSpec((1, LANES), const),
            pl.BlockSpec((1, LANES), const),
            pl.BlockSpec((1, SSD_D_INNER), const),
            pl.BlockSpec((1, SSD_D_INNER), const),
            pl.BlockSpec((3 * LANES, SSD_D_INNER), const),
            pl.BlockSpec((256, 256), const),
        ],
        out_specs=pl.BlockSpec((L, SSD_D_INNER), lambda b, n: (row(b, n), 0)),
        out_shape=jax.ShapeDtypeStruct((t, SSD_D_INNER), BF16),
        scratch_shapes=[pltpu.VMEM((8, SSD_CONV_CH), F32),
                        pltpu.VMEM((SSD_GROUPS, SSD_D_STATE, SSD_GROUP_W), F32)],
        compiler_params=_compiler_params(("parallel", "arbitrary"), vmem),
        name="ssd",
    )(proj, proj, proj, proj, proj, conv_w, conv_b.reshape(1, -1),
      jnp.pad(dt_bias, (0, pad)).reshape(1, LANES), jnp.pad(a_log, (0, pad)).reshape(1, LANES),
      jnp.repeat(d_skip, SSD_HEAD_DIM).reshape(1, -1), norm_g.reshape(1, -1), expand, blockmask)


def _regroup_cast_kernel(w_ref, o_ref, *, segments):
    o_ref[...] = jnp.zeros_like(o_ref)
    w = w_ref[...]
    for src, width, dst in segments:
        o_ref[:, dst:dst + width] = w[:, src:src + width].astype(o_ref.dtype)


def _regroup_cast(w, segments, out_w, tr=128):
    rows, n = w.shape
    vmem = 2 * tr * n * 4 + 2 * tr * out_w * 2 + tr * n * 4
    return pl.pallas_call(
        functools.partial(_regroup_cast_kernel, segments=segments),
        grid=(rows // tr,),
        in_specs=[pl.BlockSpec((tr, n), lambda i: (i, 0))],
        out_specs=pl.BlockSpec((tr, out_w), lambda i: (i, 0)),
        out_shape=jax.ShapeDtypeStruct((rows, out_w), BF16),
        compiler_params=_compiler_params(("parallel",), vmem),
        name="regroup_cast",
    )(w)


def _hyb_weight(w_in):
    glr_end = HYB_SQ + GLA_GATE_RANK
    segments = ((0, HYB_SQ, 0), (glr_end, w_in.shape[1] - glr_end, HYB_SQ), (HYB_SQ, GLA_GATE_RANK, HYB_GLR))
    return _regroup_cast(w_in, segments, HYB_W)


def _ssd_weight(w_in):
    return _regroup_cast(w_in, ((0, w_in.shape[1], 0),), SSD_W)


def _gla_swa_mixer(h, pos_col, norm_gain, w_in, gate_w2, gate_b, gla_norm, q_norm, k_norm, sinks, w_out, batch, seq):
    proj = _norm_matmul(h, norm_gain, _hyb_weight(w_in), tn=1536)
    w2p = jnp.pad(gate_w2, ((0, LANES - GLA_GATE_RANK), (0, 0))).astype(BF16)
    o_gla = _gla(proj, w2p, gate_b, gla_norm, batch, seq)
    o_swa = _swa(proj, pos_col, sinks, q_norm, k_norm, batch, seq)
    return _matmul_residual([o_gla, o_swa], w_out.astype(BF16), h)


def _ssd_mixer(h, norm_gain, w_in, conv_w, conv_b, dt_bias, a_log, d_skip, norm_g, w_out, batch, seq):
    proj = _norm_matmul(h, norm_gain, _ssd_weight(w_in), tn=1536)
    y = _ssd(proj, conv_w, conv_b, dt_bias, a_log, d_skip, norm_g, batch, seq)
    return _matmul_residual([y], w_out.astype(BF16), h)


def kernel(x, positions, norm_ffn, w_ffn_gu, w_ffn_down, norm_mix, hyb_w_in, gla_gate_w2, gla_gate_b, gla_norm,
           attn_q_norm, attn_k_norm, attn_sinks, hyb_w_out, ssd_w_in, ssd_conv_w, ssd_conv_b, ssd_dt_bias, ssd_a_log,
           ssd_d, ssd_norm, ssd_w_out):
    batch, seq, d = x.shape
    h = x.reshape(batch * seq, d)
    pos_col = positions.reshape(batch * seq, 1)
    depth = norm_ffn.shape[0]
    ffns = [(layer, pos) for layer in range(depth) for pos in range(2)]
    weights = (w_ffn_gu[0, 0].astype(BF16), w_ffn_down[0, 0].astype(BF16))
    for layer in range(depth):
        i = layer // 2
        for pos in range(2):
            if pos == 1:
                if layer % 2 == 0:
                    h = _gla_swa_mixer(h, pos_col, norm_mix[layer], hyb_w_in[i], gla_gate_w2[i], gla_gate_b[i],
                                       gla_norm[i], attn_q_norm[i], attn_k_norm[i], attn_sinks[i], hyb_w_out[i],
                                       batch, seq)
                else:
                    h = _ssd_mixer(h, norm_mix[layer], ssd_w_in[i], ssd_conv_w[i], ssd_conv_b[i], ssd_dt_bias[i],
                                   ssd_a_log[i], ssd_d[i], ssd_norm[i], ssd_w_out[i], batch, seq)
            k = ffns.index((layer, pos))
            nxt = (w_ffn_gu, w_ffn_down) + ffns[k + 1] if k + 1 < len(ffns) else None
            h, weights = _ffn(h, norm_ffn[layer, pos], weights[0], weights[1], nxt)
    return h.reshape(batch, seq, d)
```

```python
import functools

import numpy as np
import jax
import jax.numpy as jnp
from jax import lax
from jax.experimental import pallas as pl
from jax.experimental.pallas import tpu as pltpu

F32 = jnp.float32
BF16 = jnp.bfloat16

D_MODEL = 2048
D_FF = 5632
NORM_EPS = 1e-6
MACARON_WEIGHT = 0.5

GLA_HEADS = 4
GLA_DK = 128
GLA_DV = 256
GLA_GATE_RANK = 16
GLA_TAU = 16.0
GLA_CHUNK = 64

SWA_HEAD_DIM = 64
SWA_Q_HEADS = 16
SWA_KV_HEADS = 2
SWA_WINDOW = 128
SWA_BLOCK = 128
ROPE_THETA = 500000.0
ROPE_DIM = 16

SSD_D_INNER = 4096
SSD_HEAD_DIM = 64
SSD_HEADS = 64
SSD_GROUPS = 8
SSD_D_STATE = 128
SSD_CONV = 4
SSD_CHUNK = 64
SSD_GROUP_W = SSD_D_INNER // SSD_GROUPS
SSD_BC_W = SSD_GROUPS * SSD_D_STATE
SSD_CONV_CH = SSD_D_INNER + 2 * SSD_BC_W

LANES = 128
SUBLANES = 8
MXU_TILE = 256
VMEM_LIMIT_CAP = 60 * 1024 * 1024

HYB_Q, HYB_K, HYB_V, HYB_R, HYB_SQ, HYB_SK, HYB_SV, HYB_GLR = 0, 512, 1024, 2048, 3072, 4096, 4224, 4352
HYB_W = 4608
SSD_Z, SSD_X, SSD_B, SSD_C, SSD_DT = 0, 4096, 8192, 9216, 10240
SSD_W = 10752


def _compiler_params(semantics, vmem_bytes):
    limit = min(int(vmem_bytes * 1.25) + (4 << 20), VMEM_LIMIT_CAP)
    return pltpu.CompilerParams(dimension_semantics=semantics, vmem_limit_bytes=limit)


def _rms_norm(x, gain):
    ms = jnp.mean(x * x, axis=-1, keepdims=True)
    return x * lax.rsqrt(ms + NORM_EPS) * gain


def _silu(x):
    return x * jax.nn.sigmoid(x)


def _softplus(x):
    return jnp.maximum(x, 0.0) + jnp.log1p(jnp.exp(-jnp.abs(x)))


def _split_bf16(x, n):
    parts, rest = [], x
    for _ in range(n):
        p = rest.astype(BF16)
        parts.append(p)
        rest = rest - p.astype(F32)
    return parts


def _dot(a, b):
    return jnp.dot(a, b, preferred_element_type=F32)


def _dot_nt(a, b):
    return lax.dot_general(a, b, (((1,), (1,)), ((), ())), preferred_element_type=F32)


def _select_dot_lhs(sel, x, n):
    return sum(_dot(sel, p) for p in _split_bf16(x, n))


def _ffn_kernel(x_ref, g_ref, wg_ref, wu_ref, wd_ref, *rest, cast_next):
    if cast_next:
        ngu_ref, nd_ref, o_ref, ogu_ref, od_ref, hn_ref = rest
    else:
        o_ref, hn_ref = rest

    @pl.when(pl.program_id(1) == 0)
    def _():
        x = x_ref[...]
        hn_ref[...] = _rms_norm(x, g_ref[...]).astype(BF16)
        o_ref[...] = x

    h = hn_ref[...]
    gate = _dot(h, wg_ref[...])
    up = _dot(h, wu_ref[...])
    act = (_silu(gate) * (up * MACARON_WEIGHT)).astype(BF16)
    o_ref[...] += _dot(act, wd_ref[...])
    if cast_next:
        ogu_ref[...] = ngu_ref[...].astype(BF16)
        od_ref[...] = nd_ref[...].astype(BF16)


def _ffn(h, gain, w_gu, w_down, next_weights=None, tm=1024, tf=512):
    t, d = h.shape
    nf = D_FF // tf
    ni = t // tm
    vmem = 2 * (2 * tm * d * 4) + 2 * 3 * d * tf * 2 + tm * d * 2 + 3 * tm * tf * 4
    in_specs = [
        pl.BlockSpec((tm, d), lambda i, j: (i, 0)),
        pl.BlockSpec((1, d), lambda i, j: (0, 0)),
        pl.BlockSpec((d, tf), lambda i, j: (0, j)),
        pl.BlockSpec((d, tf), lambda i, j: (0, j + nf)),
        pl.BlockSpec((tf, d), lambda i, j: (j, 0)),
    ]
    out_specs = [pl.BlockSpec((tm, d), lambda i, j: (i, 0))]
    out_shape = [jax.ShapeDtypeStruct((t, d), F32)]
    args = [h, gain.reshape(1, d), w_gu, w_gu, w_down]
    if next_weights is not None:
        w_gu_all, w_down_all, layer, pos = next_weights
        gu_blk = (d // ni, 2 * D_FF // nf)
        dn_blk = (D_FF // nf, d // ni)
        in_specs += [
            pl.BlockSpec((None, None) + gu_blk, lambda i, j: (layer, pos, i, j)),
            pl.BlockSpec((None, None) + dn_blk, lambda i, j: (layer, pos, j, i)),
        ]
        out_specs += [pl.BlockSpec(gu_blk, lambda i, j: (i, j)), pl.BlockSpec(dn_blk, lambda i, j: (j, i))]
        out_shape += [jax.ShapeDtypeStruct((d, 2 * D_FF), BF16), jax.ShapeDtypeStruct((D_FF, d), BF16)]
        args += [w_gu_all, w_down_all]
        vmem += 2 * (gu_blk[0] * gu_blk[1] + dn_blk[0] * dn_blk[1]) * (4 + 2)
    outs = pl.pallas_call(
        functools.partial(_ffn_kernel, cast_next=next_weights is not None),
        grid=(ni, nf),
        in_specs=in_specs,
        out_specs=out_specs,
        out_shape=out_shape,
        scratch_shapes=[pltpu.VMEM((tm, d), BF16)],
        compiler_params=_compiler_params(("parallel", "arbitrary"), vmem),
        name="ffn",
    )(*args)
    return outs[0], (tuple(outs[1:]) if next_weights is not None else None)


def _norm_matmul_kernel(x_ref, g_ref, w_ref, o_ref, hn_ref):
    @pl.when(pl.program_id(1) == 0)
    def _():
        hn_ref[...] = _rms_norm(x_ref[...], g_ref[...]).astype(BF16)

    o_ref[...] = _dot(hn_ref[...], w_ref[...]).astype(o_ref.dtype)


def _norm_matmul(h, gain, w, tn, tm=1024):
    t, d = h.shape
    n = w.shape[1]
    vmem = 2 * tm * d * 4 + 2 * d * tn * 2 + 2 * tm * tn * 2 + tm * d * 2 + tm * tn * 4
    return pl.pallas_call(
        _norm_matmul_kernel,
        grid=(t // tm, n // tn),
        in_specs=[
            pl.BlockSpec((tm, d), lambda i, j: (i, 0)),
            pl.BlockSpec((1, d), lambda i, j: (0, 0)),
            pl.BlockSpec((d, tn), lambda i, j: (0, j)),
        ],
        out_specs=pl.BlockSpec((tm, tn), lambda i, j: (i, j)),
        out_shape=jax.ShapeDtypeStruct((t, n), BF16),
        scratch_shapes=[pltpu.VMEM((tm, d), BF16)],
        compiler_params=_compiler_params(("parallel", "arbitrary"), vmem),
        name="norm_matmul",
    )(h, gain.reshape(1, d), w)


def _matmul_residual_kernel(*refs, n_in):
    a_refs, w_refs = refs[:n_in], refs[n_in:2 * n_in]
    r_ref, o_ref = refs[2 * n_in], refs[2 * n_in + 1]
    acc = r_ref[...]
    for a_ref, w_ref in zip(a_refs, w_refs):
        acc = acc + _dot(a_ref[...], w_ref[...])
    o_ref[...] = acc


def _matmul_residual(acts, w, res, tm=512):
    t, d = res.shape
    n_in = len(acts)
    k = acts[0].shape[1]
    assert all(a.shape[1] == k for a in acts) and w.shape == (n_in * k, d)
    vmem = 2 * 2 * tm * d * 4 + n_in * (2 * tm * k * 2 + 2 * k * d * 2)
    in_specs = [pl.BlockSpec((tm, k), lambda i: (i, 0)) for _ in acts]
    in_specs += [pl.BlockSpec((k, d), functools.partial(lambda i, idx: (idx, 0), idx=idx)) for idx in range(n_in)]
    weights = [w] * n_in
    in_specs.append(pl.BlockSpec((tm, d), lambda i: (i, 0)))
    return pl.pallas_call(
        functools.partial(_matmul_residual_kernel, n_in=n_in),
        grid=(t // tm,),
        in_specs=in_specs,
        out_specs=pl.BlockSpec((tm, d), lambda i: (i, 0)),
        out_shape=jax.ShapeDtypeStruct((t, d), F32),
        compiler_params=_compiler_params(("parallel",), vmem),
        name="matmul_residual",
    )(*acts, *weights, res)


def _gla_kernel(q_ref, k_ref, v_ref, r_ref, glr_ref, w2_ref, gb_ref, gn_ref, o_ref, st_ref, *, n_chunks):
    @pl.when(pl.program_id(1) == 0)
    def _():
        st_ref[...] = jnp.zeros_like(st_ref)

    c = GLA_CHUNK
    lb = n_chunks * c
    ri = lax.broadcasted_iota(jnp.int32, (lb, lb), 0)
    ci = lax.broadcasted_iota(jnp.int32, (lb, lb), 1)
    same_chunk_tril = (ri >= ci) & ((ri // c) == (ci // c))
    tril = jnp.where(same_chunk_tril, 1.0, 0.0).astype(BF16)
    causal = lax.broadcasted_iota(jnp.int32, (c, c), 0) >= lax.broadcasted_iota(jnp.int32, (c, c), 1)
    gn = gn_ref[...]

    z = _dot(glr_ref[...], w2_ref[...]) + gb_ref[...]
    log_a = (jnp.minimum(z, 0.0) - jnp.log1p(jnp.exp(-jnp.abs(z)))) * (1.0 / GLA_TAU)
    bcum = _select_dot_lhs(tril, log_a, 3)
    q_all = q_ref[...].astype(F32) * (GLA_DK ** -0.5) * jnp.exp(bcum)
    k_all = k_ref[...].astype(F32)
    k_in_all = k_all * jnp.exp(-bcum)

    items = [(h, idx) for h in range(GLA_HEADS) for idx in range(n_chunks)]
    kcols = lambda h: slice(h * GLA_DK, (h + 1) * GLA_DK)
    vcols = lambda h: slice(h * GLA_DV, (h + 1) * GLA_DV)
    rows = lambda idx: slice(idx * c, (idx + 1) * c)
    q_in, k_in, k_end, decay_end, v, v_t = {}, {}, {}, {}, {}, {}
    for h, idx in items:
        b_c = bcum[rows(idx), kcols(h)]
        b_last = b_c[c - 1:c, :]
        q_in[h, idx] = q_all[rows(idx), kcols(h)].astype(BF16)
        k_in[h, idx] = k_in_all[rows(idx), kcols(h)].astype(BF16)
        k_end[h, idx] = (k_all[rows(idx), kcols(h)] * jnp.exp(b_last - b_c)).astype(BF16)
        decay_end[h, idx] = jnp.exp(b_last)
        v[h, idx] = v_ref[rows(idx), vcols(h)]
        v_t[h, idx] = v[h, idx].astype(F32).T.astype(BF16)
    attn = {it: jnp.where(causal, _dot_nt(q_in[it], k_in[it]), 0.0).astype(BF16) for it in items}
    o_intra = {it: _dot(attn[it], v[it]) for it in items}
    upd = {it: _dot(v_t[it], k_end[it]) for it in items}
    for h in range(GLA_HEADS):
        state = st_ref[h]
        for idx in range(n_chunks):
            o = o_intra[h, idx] + _dot_nt(q_in[h, idx], state.astype(BF16))
            state = decay_end[h, idx] * state + upd[h, idx]
            o = _rms_norm(o, gn) * _silu(r_ref[rows(idx), vcols(h)].astype(F32))
            o_ref[rows(idx), vcols(h)] = o.astype(o_ref.dtype)
        st_ref[h] = state


def _gla(proj, w2p, gate_b, gla_norm, batch, seq, lb=256):
    t = batch * seq
    ns = seq // lb
    qk_w, v_w = GLA_HEADS * GLA_DK, GLA_HEADS * GLA_DV
    row = lambda b, n: b * ns + n
    vmem = 2 * lb * (2 * qk_w + 2 * v_w + LANES) * 2 + 2 * lb * v_w * 2 + 8 * lb * qk_w * 4 + (4 << 20)
    return pl.pallas_call(
        functools.partial(_gla_kernel, n_chunks=lb // GLA_CHUNK),
        grid=(batch, ns),
        in_specs=[
            pl.BlockSpec((lb, qk_w), lambda b, n: (row(b, n), HYB_Q // qk_w)),
            pl.BlockSpec((lb, qk_w), lambda b, n: (row(b, n), HYB_K // qk_w)),
            pl.BlockSpec((lb, v_w), lambda b, n: (row(b, n), HYB_V // v_w)),
            pl.BlockSpec((lb, v_w), lambda b, n: (row(b, n), HYB_R // v_w)),
            pl.BlockSpec((lb, LANES), lambda b, n: (row(b, n), HYB_GLR // LANES)),
            pl.BlockSpec((LANES, qk_w), lambda b, n: (0, 0)),
            pl.BlockSpec((1, qk_w), lambda b, n: (0, 0)),
            pl.BlockSpec((1, GLA_DV), lambda b, n: (0, 0)),
        ],
        out_specs=pl.BlockSpec((lb, v_w), lambda b, n: (row(b, n), 0)),
        out_shape=jax.ShapeDtypeStruct((t, v_w), BF16),
        scratch_shapes=[pltpu.VMEM((GLA_HEADS, GLA_DV, GLA_DK), F32)],
        compiler_params=_compiler_params(("parallel", "arbitrary"), vmem),
        name="gla",
    )(proj, proj, proj, proj, proj, w2p, gate_b.reshape(1, -1), gla_norm.reshape(1, -1))


def _swa_kernel(sink_ref, pos_ref, q_ref, kv_ref, qn_ref, kn_ref, freq_ref, bd_ref, o_ref, kprev_ref, vprev_ref):
    n = pl.program_id(1)
    w = SWA_BLOCK
    lane = lax.broadcasted_iota(jnp.int32, (w, LANES), 1)
    lane_lo = lane < SWA_HEAD_DIM
    lane_lo_kv = lax.broadcasted_iota(jnp.int32, (2 * w, LANES), 1) < SWA_HEAD_DIM
    dim = lane & (SWA_HEAD_DIM - 1)
    half = ROPE_DIM // 2
    bd = bd_ref[...]

    @pl.when(n == 0)
    def _():
        kprev_ref[...] = jnp.zeros_like(kprev_ref)
        vprev_ref[...] = jnp.zeros_like(vprev_ref)

    ang = pos_ref[...].astype(F32) * freq_ref[...]
    cos, sin = jnp.cos(ang), jnp.sin(ang)
    s_hi = jnp.where((dim >= half) & (dim < ROPE_DIM), sin, 0.0)
    s_lo = jnp.where(dim < half, -sin, 0.0)

    def norm_rope(x, gain):
        ms = _dot((x * x).astype(BF16), bd) * (1.0 / SWA_HEAD_DIM)
        xn = x * lax.rsqrt(ms + NORM_EPS) * gain
        return xn * cos + pltpu.roll(xn, half, 1) * s_hi + pltpu.roll(xn, LANES - half, 1) * s_lo

    k_cur = norm_rope(kv_ref[:, 0:LANES].astype(F32), kn_ref[...])
    v_cur = kv_ref[:, LANES:2 * LANES].astype(F32)
    kk = jnp.concatenate([kprev_ref[...], k_cur], axis=0)
    vv = jnp.concatenate([vprev_ref[...], v_cur], axis=0)
    kprev_ref[...] = k_cur
    vprev_ref[...] = v_cur
    kk_sw = pltpu.roll(kk, SWA_HEAD_DIM, 1)
    vv_sw = pltpu.roll(vv, SWA_HEAD_DIM, 1)
    k2 = [jnp.where(lane_lo_kv, kk, kk_sw).astype(BF16), jnp.where(lane_lo_kv, kk_sw, kk).astype(BF16)]
    v2 = [jnp.where(lane_lo_kv, vv, vv_sw).astype(BF16), jnp.where(lane_lo_kv, vv_sw, vv).astype(BF16)]

    qi = lax.broadcasted_iota(jnp.int32, (w, 2 * w), 0)
    ki = lax.broadcasted_iota(jnp.int32, (w, 2 * w), 1)
    rel = qi + w - ki
    valid = (rel >= 0) & (rel < SWA_WINDOW) & ((ki >= w) | (n > 0))

    qn = qn_ref[...] * (SWA_HEAD_DIM ** -0.5)
    heads_per_kv = SWA_Q_HEADS // SWA_KV_HEADS
    pairs = range(SWA_Q_HEADS // 2)
    heads = [(pair, hf) for pair in pairs for hf in range(2)]
    qp = [norm_rope(q_ref[:, pl.ds(pair * LANES, LANES)].astype(F32), qn) for pair in pairs]
    scores = []
    for pair, hf in heads:
        keep = lane_lo if hf == 0 else jnp.logical_not(lane_lo)
        qm = jnp.where(keep, qp[pair], 0.0).astype(BF16)
        scores.append(jnp.where(valid, _dot_nt(qm, k2[(2 * pair) // heads_per_kv]), -1e30))
    probs, rdenom = [], []
    for (pair, hf), s in zip(heads, scores):
        sink = sink_ref[2 * pair + hf]
        m = jnp.maximum(jnp.max(s, axis=-1, keepdims=True), sink)
        p = jnp.exp(s - m)
        rdenom.append(1.0 / (jnp.sum(p, axis=-1, keepdims=True) + jnp.exp(sink - m)))
        probs.append(p.astype(BF16))
    outs = [_dot(p, v2[(2 * pair) // heads_per_kv]) * r for (pair, hf), p, r in zip(heads, probs, rdenom)]
    for pair in pairs:
        o_ref[:, pl.ds(pair * LANES, LANES)] = jnp.where(lane_lo, outs[2 * pair], outs[2 * pair + 1]).astype(o_ref.dtype)


def _swa(proj, pos_col, sinks, q_norm, k_norm, batch, seq):
    t = batch * seq
    w = SWA_BLOCK
    nb = seq // w
    half = ROPE_DIM // 2
    inv_freq = ROPE_THETA ** (-2.0 * jnp.arange(half, dtype=F32) / ROPE_DIM)
    dim = np.arange(LANES) % SWA_HEAD_DIM
    freq_row = jnp.where(dim < ROPE_DIM, inv_freq[dim % half], 0.0).reshape(1, LANES).astype(F32)
    blockdiag = jnp.asarray(np.kron(np.eye(LANES // SWA_HEAD_DIM), np.ones((SWA_HEAD_DIM, SWA_HEAD_DIM))), BF16)
    cur = lambda b, n: b * nb + n
    q_w = SWA_Q_HEADS * SWA_HEAD_DIM
    vmem = 2 * w * (q_w + 2 * LANES) * 2 + 2 * w * q_w * 2 + 4 * w * LANES * 4 + (8 << 20)
    return pl.pallas_call(
        _swa_kernel,
        grid=(batch, nb),
        in_specs=[
            pl.BlockSpec(memory_space=pltpu.SMEM),
            pl.BlockSpec((w, 1), lambda b, n: (cur(b, n), 0)),
            pl.BlockSpec((w, q_w), lambda b, n: (cur(b, n), HYB_SQ // q_w)),
            pl.BlockSpec((w, 2 * LANES), lambda b, n: (cur(b, n), HYB_SK // (2 * LANES))),
            pl.BlockSpec((1, LANES), lambda b, n: (0, 0)),
            pl.BlockSpec((1, LANES), lambda b, n: (0, 0)),
            pl.BlockSpec((1, LANES), lambda b, n: (0, 0)),
            pl.BlockSpec((LANES, LANES), lambda b, n: (0, 0)),
        ],
        out_specs=pl.BlockSpec((w, q_w), lambda b, n: (cur(b, n), 0)),
        out_shape=jax.ShapeDtypeStruct((t, q_w), BF16),
        scratch_shapes=[pltpu.VMEM((w, LANES), F32), pltpu.VMEM((w, LANES), F32)],
        compiler_params=_compiler_params(("parallel", "arbitrary"), vmem),
        name="swa",
    )(sinks.astype(F32), pos_col, proj, proj,
      jnp.tile(q_norm, 2).reshape(1, LANES), jnp.tile(k_norm, 2).reshape(1, LANES), freq_row, blockdiag)


def _ssd_kernel(x_ref, b_ref, c_ref, z_ref, dt_ref, cw_ref, cb_ref, dtb_ref, alog_ref, dsk_ref, ng_ref, e_ref, bm_ref,
                o_ref, tail_ref, st_ref, *, n_chunks):
    n = pl.program_id(1)
    L = SSD_CHUNK
    lb = n_chunks * L
    gw = SSD_GROUP_W
    hp = SSD_HEAD_DIM
    ns = SSD_D_STATE
    groups = range(SSD_GROUPS)
    chunks = range(n_chunks)
    crow = lambda c: slice(c * L, (c + 1) * L)

    @pl.when(n == 0)
    def _():
        tail_ref[...] = jnp.zeros_like(tail_ref)
        st_ref[...] = jnp.zeros_like(st_ref)

    cur = jnp.concatenate([x_ref[...], b_ref[...], c_ref[...]], axis=1).astype(F32)
    ext = jnp.concatenate([tail_ref[...], cur], axis=0)
    tail_ref[...] = cur[lb - SUBLANES:lb, :]
    acc = cb_ref[...] + cw_ref[SSD_CONV - 1:SSD_CONV, :] * cur
    for j in range(SSD_CONV - 1):
        shifted = pltpu.roll(ext, SSD_CONV - 1 - j, 0)[SUBLANES:SUBLANES + lb, :]
        acc = acc + cw_ref[j:j + 1, :] * shifted
    xbc = _silu(acc)

    lane = lax.broadcasted_iota(jnp.int32, (1, LANES), 1)
    dt = _softplus(dt_ref[...].astype(F32) + dtb_ref[...])
    a_neg = jnp.where(lane < SSD_HEADS, -jnp.exp(alog_ref[...]), 0.0)
    ri = lax.broadcasted_iota(jnp.int32, (lb, lb), 0)
    ci = lax.broadcasted_iota(jnp.int32, (lb, lb), 1)
    tril = jnp.where((ri >= ci) & ((ri // L) == (ci // L)), 1.0, 0.0).astype(BF16)
    acum = _select_dot_lhs(tril, dt * a_neg, 3)
    wgt_small = jnp.concatenate([jnp.exp(acum[c * L + L - 1:c * L + L, :] - acum[crow(c), :]) for c in chunks], axis=0)
    acum_cat = jnp.concatenate(_split_bf16(acum, 3), axis=1)
    dt_cat = jnp.concatenate(_split_bf16(dt, 2), axis=1)
    eacum_cat = jnp.concatenate(_split_bf16(jnp.exp(acum), 2), axis=1)
    wgt_cat = jnp.concatenate(_split_bf16(wgt_small, 2), axis=1)
    acum_t = [jnp.concatenate([acum[crow(c), :], acum[crow(c), :]], axis=0).T for c in chunks]

    lane_lo = lane < hp
    row_l = lax.broadcasted_iota(jnp.int32, (L, gw), 0)
    src_s = lax.broadcasted_iota(jnp.int32, (L, gw), 1) & (hp - 1)
    causal = src_s <= row_l
    blockmask = bm_ref[...]
    heads_per_group = SSD_HEADS // SSD_GROUPS

    xs, b_bf, c_bf = [], [], []
    for g in groups:
        xs.append(xbc[:, g * gw:(g + 1) * gw])
        b_bf.append(xbc[:, SSD_D_INNER + g * ns:SSD_D_INNER + (g + 1) * ns])
        c_bf.append(xbc[:, SSD_D_INNER + SSD_BC_W + g * ns:SSD_D_INNER + SSD_BC_W + (g + 1) * ns].astype(BF16))
    b_t = [[b_bf[g][crow(c), :].T.astype(BF16) for g in groups] for c in chunks]
    b_bf = [b.astype(BF16) for b in b_bf]

    col, dt_exp, ecol, wgt = [], [], [], []
    for g in groups:
        e_g = e_ref[:, pl.ds(g * gw, gw)]
        col.append(_dot(acum_cat, e_g))
        dt_exp.append(_dot(dt_cat, e_g[:2 * LANES, :]))
        ecol.append(_dot(eacum_cat, e_g[:2 * LANES, :]))
        wgt.append(_dot(wgt_cat, e_g[:2 * LANES, :]))
    cb = [[_dot_nt(c_bf[g][crow(c), :], b_bf[g][crow(c), :]) for g in groups] for c in chunks]

    xdt = [xs[g] * dt_exp[g] for g in groups]
    xdt_bf = [t.astype(BF16) for t in xdt]
    xw = [(xdt[g] * wgt[g]).astype(BF16) for g in groups]
    m = []
    for c in chunks:
        m_c = []
        for g in groups:
            row_parts = []
            for pr in range(heads_per_group // 2):
                h0 = g * heads_per_group + 2 * pr
                r = jnp.where(lane_lo, acum_t[c][h0:h0 + 1, :], acum_t[c][h0 + 1:h0 + 2, :])
                row_parts.append(jnp.broadcast_to(r, (L, LANES)))
            row = jnp.concatenate(row_parts, axis=1)
            decay = jnp.where(causal, jnp.exp(col[g][crow(c), :] - row), 0.0)
            cb2 = jnp.concatenate([cb[c][g], cb[c][g]], axis=1)
            cb_exp = jnp.concatenate([cb2] * (gw // LANES), axis=1)
            m_c.append((cb_exp * decay).astype(BF16))
        m.append(m_c)

    y_diag, upd = [], []
    for c in chunks:
        yd_c, upd_c = [], []
        for g in groups:
            y_parts = []
            for qd in range(gw // MXU_TILE):
                sl = slice(qd * MXU_TILE, (qd + 1) * MXU_TILE)
                rhs = jnp.concatenate([xdt_bf[g][crow(c), sl]] * (MXU_TILE // hp), axis=0) * blockmask
                y_parts.append(_dot(m[c][g][:, sl], rhs))
            yd_c.append(jnp.concatenate(y_parts, axis=1))
            upd_c.append(_dot(b_t[c][g], xw[g][crow(c), :]))
        y_diag.append(yd_c)
        upd.append(upd_c)

    for g in groups:
        cols = pl.ds(g * gw, gw)
        state = st_ref[g]
        y_g = []
        for c in chunks:
            y_off = _dot(c_bf[g][crow(c), :], state.astype(BF16)) * ecol[g][crow(c), :]
            y_g.append(y_diag[c][g] + y_off)
            state = ecol[g][c * L + L - 1:c * L + L, :] * state + upd[c][g]
        st_ref[g] = state
        y = jnp.concatenate(y_g, axis=0) + xs[g] * dsk_ref[:, cols]
        y = y * _silu(z_ref[:, cols].astype(F32))
        o_ref[:, cols] = _rms_norm(y, ng_ref[:, cols]).astype(o_ref.dtype)


def _ssd(proj, conv_w, conv_b, dt_bias, a_log, d_skip, norm_g, batch, seq, n_chunks=2):
    t = batch * seq
    L = n_chunks * SSD_CHUNK
    ns = seq // L
    pad = LANES - SSD_HEADS
    expand = np.kron(np.eye(LANES, SSD_HEADS), np.ones((1, SSD_HEAD_DIM)))
    expand = jnp.asarray(np.tile(expand, (3, 1)), BF16)
    blockmask = jnp.asarray(np.kron(np.eye(MXU_TILE // SSD_HEAD_DIM), np.ones((SSD_HEAD_DIM, SSD_HEAD_DIM))), BF16)
    row = lambda b, n: b * ns + n
    const = lambda b, n: (0, 0)
    vmem = (2 * L * (SSD_CONV_CH + SSD_D_INNER + LANES) * 2 + 2 * L * SSD_D_INNER * 2
            + 2 * (6 * SSD_CONV_CH * 4 + 3 * LANES * SSD_D_INNER * 2)
            + SUBLANES * SSD_CONV_CH * 4 + SSD_GROUPS * SSD_D_STATE * SSD_GROUP_W * 4 + (12 << 20))
    return pl.pallas_call(
        functools.partial(_ssd_kernel, n_chunks=n_chunks),
        grid=(batch, ns),
        in_specs=[
            pl.BlockSpec((L, SSD_D_INNER), lambda b, n: (row(b, n), SSD_X // SSD_D_INNER)),
            pl.BlockSpec((L, SSD_BC_W), lambda b, n: (row(b, n), SSD_B // SSD_BC_W)),
            pl.BlockSpec((L, SSD_BC_W), lambda b, n: (row(b, n), SSD_C // SSD_BC_W)),
            pl.BlockSpec((L, SSD_D_INNER), lambda b, n: (row(b, n), SSD_Z // SSD_D_INNER)),
            pl.BlockSpec((L, LANES), lambda b, n: (row(b, n), SSD_DT // LANES)),
            pl.BlockSpec((SSD_CONV, SSD_CONV_CH), const),
            pl.BlockSpec((1, SSD_CONV_CH), const),
            pl.BlockSpec((1, LANES), const),
            pl.BlockSpec((1, LANES), const),
            pl.BlockSpec((1, SSD_D_INNER), const),
            pl.BlockSpec((1, SSD_D_INNER), const),
            pl.BlockSpec((3 * LANES, SSD_D_INNER), const),
            pl.BlockSpec((MXU_TILE, MXU_TILE), const),
        ],
        out_specs=pl.BlockSpec((L, SSD_D_INNER), lambda b, n: (row(b, n), 0)),
        out_shape=jax.ShapeDtypeStruct((t, SSD_D_INNER), BF16),
        scratch_shapes=[pltpu.VMEM((SUBLANES, SSD_CONV_CH), F32),
                        pltpu.VMEM((SSD_GROUPS, SSD_D_STATE, SSD_GROUP_W), F32)],
        compiler_params=_compiler_params(("parallel", "arbitrary"), vmem),
        name="ssd",
    )(proj, proj, proj, proj, proj, conv_w, conv_b.reshape(1, -1),
      jnp.pad(dt_bias, (0, pad)).reshape(1, LANES), jnp.pad(a_log, (0, pad)).reshape(1, LANES),
      jnp.repeat(d_skip, SSD_HEAD_DIM).reshape(1, -1), norm_g.reshape(1, -1), expand, blockmask)


def _hyb_weight(w_in):
    w = w_in.astype(BF16)
    glr_end = HYB_SQ + GLA_GATE_RANK
    zeros = jnp.zeros((w.shape[0], HYB_W - HYB_GLR - GLA_GATE_RANK), BF16)
    return jnp.concatenate([w[:, :HYB_SQ], w[:, glr_end:], w[:, HYB_SQ:glr_end], zeros], axis=1)


def _ssd_weight(w_in):
    return jnp.pad(w_in, ((0, 0), (0, SSD_W - w_in.shape[1]))).astype(BF16)


def _gla_swa_mixer(h, pos_col, norm_gain, w_in, gate_w2, gate_b, gla_norm, q_norm, k_norm, sinks, w_out, batch, seq):
    proj = _norm_matmul(h, norm_gain, _hyb_weight(w_in), tn=1536)
    w2p = jnp.pad(gate_w2, ((0, LANES - GLA_GATE_RANK), (0, 0))).astype(BF16)
    o_gla = _gla(proj, w2p, gate_b, gla_norm, batch, seq)
    o_swa = _swa(proj, pos_col, sinks, q_norm, k_norm, batch, seq)
    return _matmul_residual([o_gla, o_swa], w_out.astype(BF16), h)


def _ssd_mixer(h, norm_gain, w_in, conv_w, conv_b, dt_bias, a_log, d_skip, norm_g, w_out, batch, seq):
    proj = _norm_matmul(h, norm_gain, _ssd_weight(w_in), tn=1536)
    y = _ssd(proj, conv_w, conv_b, dt_bias, a_log, d_skip, norm_g, batch, seq)
    return _matmul_residual([y], w_out.astype(BF16), h)


def kernel(x, positions, norm_ffn, w_ffn_gu, w_ffn_down, norm_mix, hyb_w_in, gla_gate_w2, gla_gate_b, gla_norm,
           attn_q_norm, attn_k_norm, attn_sinks, hyb_w_out, ssd_w_in, ssd_conv_w, ssd_conv_b, ssd_dt_bias, ssd_a_log,
           ssd_d, ssd_norm, ssd_w_out):
    batch, seq, d = x.shape
    h = x.reshape(batch * seq, d)
    pos_col = positions.reshape(batch * seq, 1)
    depth = norm_ffn.shape[0]
    ffns = [(layer, pos) for layer in range(depth) for pos in range(2)]
    weights = (w_ffn_gu[0, 0].astype(BF16), w_ffn_down[0, 0].astype(BF16))
    for layer in range(depth):
        i = layer // 2
        for pos in range(2):
            if pos == 1:
                if layer % 2 == 0:
                    h = _gla_swa_mixer(h, pos_col, norm_mix[layer], hyb_w_in[i], gla_gate_w2[i], gla_gate_b[i],
                                       gla_norm[i], attn_q_norm[i], attn_k_norm[i], attn_sinks[i], hyb_w_out[i],
                                       batch, seq)
                else:
                    h = _ssd_mixer(h, norm_mix[layer], ssd_w_in[i], ssd_conv_w[i], ssd_conv_b[i], ssd_dt_bias[i],
                                   ssd_a_log[i], ssd_d[i], ssd_norm[i], ssd_w_out[i], batch, seq)
            k = ffns.index((layer, pos))
            nxt = (w_ffn_gu, w_ffn_down) + ffns[k + 1] if k + 1 < len(ffns) else None
            h, weights = _ffn(h, norm_ffn[layer, pos], weights[0], weights[1], nxt)
    return h.reshape(batch, seq, d)
```

```python
import functools

import numpy as np
import jax
import jax.numpy as jnp
from jax import lax
from jax.experimental import pallas as pl
from jax.experimental.pallas import tpu as pltpu

F32 = jnp.float32
BF16 = jnp.bfloat16

D_MODEL = 2048
D_FF = 5632
NORM_EPS = 1e-6
MACARON_WEIGHT = 0.5

GLA_HEADS = 4
GLA_DK = 128
GLA_DV = 256
GLA_GATE_RANK = 16
GLA_TAU = 16.0
GLA_CHUNK = 64

SWA_HEAD_DIM = 64
SWA_Q_HEADS = 16
SWA_KV_HEADS = 2
SWA_WINDOW = 128
SWA_BLOCK = 128
ROPE_THETA = 500000.0
ROPE_DIM = 16

SSD_D_INNER = 4096
SSD_HEAD_DIM = 64
SSD_HEADS = 64
SSD_GROUPS = 8
SSD_D_STATE = 128
SSD_CONV = 4
SSD_CHUNK = 64
SSD_GROUP_W = SSD_D_INNER // SSD_GROUPS
SSD_BC_W = SSD_GROUPS * SSD_D_STATE
SSD_CONV_CH = SSD_D_INNER + 2 * SSD_BC_W

LANES = 128
SUBLANES = 8
MXU_TILE = 256
VMEM_LIMIT_CAP = 60 * 1024 * 1024

HYB_Q, HYB_K, HYB_V, HYB_R, HYB_SQ, HYB_SK, HYB_SV, HYB_GLR = 0, 512, 1024, 2048, 3072, 4096, 4224, 4352
HYB_W = 4608
SSD_Z, SSD_X, SSD_B, SSD_C, SSD_DT = 0, 4096, 8192, 9216, 10240
SSD_W = 10752


def _compiler_params(semantics, vmem_bytes):
    limit = min(int(vmem_bytes * 1.25) + (4 << 20), VMEM_LIMIT_CAP)
    return pltpu.CompilerParams(dimension_semantics=semantics, vmem_limit_bytes=limit)


def _rms_norm(x, gain):
    ms = jnp.mean(x * x, axis=-1, keepdims=True)
    return x * lax.rsqrt(ms + NORM_EPS) * gain


def _scaled_rows(x, gain, hn_ref, rs_ref):
    ms = jnp.mean(x * x, axis=-1, keepdims=True)
    hn_ref[...] = (x * gain).astype(BF16)
    rs_ref[...] = jnp.broadcast_to(lax.rsqrt(ms + NORM_EPS), rs_ref.shape)


def _silu(x):
    return x * jax.nn.sigmoid(x)


def _softplus(x):
    return jnp.maximum(x, 0.0) + jnp.log1p(jnp.exp(-jnp.abs(x)))


def _split_bf16(x, n):
    parts, rest = [], x
    for _ in range(n):
        p = rest.astype(BF16)
        parts.append(p)
        rest = rest - p.astype(F32)
    return parts


def _dot(a, b):
    return jnp.dot(a, b, preferred_element_type=F32)


def _dot_nt(a, b):
    return lax.dot_general(a, b, (((1,), (1,)), ((), ())), preferred_element_type=F32)


def _select_dot_lhs(sel, x, n):
    return sum(_dot(sel, p) for p in _split_bf16(x, n))


def _ffn_kernel(x_ref, g_ref, wg_ref, wu_ref, wd_ref, *rest, cast_next):
    if cast_next:
        ngu_ref, nd_ref, o_ref, ogu_ref, od_ref, hn_ref, rs_ref = rest
    else:
        o_ref, hn_ref, rs_ref = rest

    @pl.when(pl.program_id(1) == 0)
    def _():
        x = x_ref[...]
        _scaled_rows(x, g_ref[...], hn_ref, rs_ref)
        o_ref[...] = x

    h = hn_ref[...]
    rs = jnp.concatenate([rs_ref[...]] * (wg_ref.shape[1] // LANES), axis=1)
    gate = _dot(h, wg_ref[...]) * rs
    up = _dot(h, wu_ref[...]) * (rs * MACARON_WEIGHT)
    act = (_silu(gate) * up).astype(BF16)
    o_ref[...] += _dot(act, wd_ref[...])
    if cast_next:
        ogu_ref[...] = ngu_ref[...].astype(BF16)
        od_ref[...] = nd_ref[...].astype(BF16)


def _ffn(h, gain, w_gu, w_down, next_weights=None, tm=1024, tf=512):
    t, d = h.shape
    nf = D_FF // tf
    ni = t // tm
    vmem = 2 * (2 * tm * d * 4) + 2 * 3 * d * tf * 2 + tm * d * 2 + 3 * tm * tf * 4
    in_specs = [
        pl.BlockSpec((tm, d), lambda i, j: (i, 0)),
        pl.BlockSpec((1, d), lambda i, j: (0, 0)),
        pl.BlockSpec((d, tf), lambda i, j: (0, j)),
        pl.BlockSpec((d, tf), lambda i, j: (0, j + nf)),
        pl.BlockSpec((tf, d), lambda i, j: (j, 0)),
    ]
    out_specs = [pl.BlockSpec((tm, d), lambda i, j: (i, 0))]
    out_shape = [jax.ShapeDtypeStruct((t, d), F32)]
    args = [h, gain.reshape(1, d), w_gu, w_gu, w_down]
    if next_weights is not None:
        w_gu_all, w_down_all, layer, pos = next_weights
        gu_blk = (d // ni, 2 * D_FF // nf)
        dn_blk = (D_FF // nf, d // ni)
        in_specs += [
            pl.BlockSpec((None, None) + gu_blk, lambda i, j: (layer, pos, i, j)),
            pl.BlockSpec((None, None) + dn_blk, lambda i, j: (layer, pos, j, i)),
        ]
        out_specs += [pl.BlockSpec(gu_blk, lambda i, j: (i, j)), pl.BlockSpec(dn_blk, lambda i, j: (j, i))]
        out_shape += [jax.ShapeDtypeStruct((d, 2 * D_FF), BF16), jax.ShapeDtypeStruct((D_FF, d), BF16)]
        args += [w_gu_all, w_down_all]
        vmem += 2 * (gu_blk[0] * gu_blk[1] + dn_blk[0] * dn_blk[1]) * (4 + 2)
    outs = pl.pallas_call(
        functools.partial(_ffn_kernel, cast_next=next_weights is not None),
        grid=(ni, nf),
        in_specs=in_specs,
        out_specs=out_specs,
        out_shape=out_shape,
        scratch_shapes=[pltpu.VMEM((tm, d), BF16), pltpu.VMEM((tm, LANES), F32)],
        compiler_params=_compiler_params(("parallel", "arbitrary"), vmem),
        name="ffn",
    )(*args)
    return outs[0], (tuple(outs[1:]) if next_weights is not None else None)


def _norm_matmul_kernel(x_ref, g_ref, w_ref, tail_ref, o_ref, hn_ref, *, n_main):
    j = pl.program_id(1)

    @pl.when(j == 0)
    def _():
        hn_ref[...] = _rms_norm(x_ref[...], g_ref[...]).astype(BF16)

    @pl.when(j < n_main)
    def _():
        o_ref[...] = _dot(hn_ref[...], w_ref[...]).astype(o_ref.dtype)

    @pl.when(j >= n_main)
    def _():
        o_ref[...] = _dot(hn_ref[...], tail_ref[...]).astype(o_ref.dtype)


def _norm_matmul(h, gain, w, tn, tm=1024):
    t, d = h.shape
    n_main = w.shape[1] // tn
    rest = w.shape[1] - n_main * tn
    tail = jnp.pad(w[:, n_main * tn:], ((0, 0), (0, tn - rest))) if rest else w[:, :tn]
    n_blocks = n_main + (1 if rest else 0)
    vmem = 2 * tm * d * 4 + 4 * d * tn * 2 + 2 * tm * tn * 2 + tm * d * 2 + tm * tn * 4
    return pl.pallas_call(
        functools.partial(_norm_matmul_kernel, n_main=n_main),
        grid=(t // tm, n_blocks),
        in_specs=[
            pl.BlockSpec((tm, d), lambda i, j: (i, 0)),
            pl.BlockSpec((1, d), lambda i, j: (0, 0)),
            pl.BlockSpec((d, tn), lambda i, j: (0, jnp.minimum(j, n_main - 1))),
            pl.BlockSpec((d, tn), lambda i, j: (0, 0)),
        ],
        out_specs=pl.BlockSpec((tm, tn), lambda i, j: (i, j)),
        out_shape=jax.ShapeDtypeStruct((t, n_blocks * tn), BF16),
        scratch_shapes=[pltpu.VMEM((tm, d), BF16)],
        compiler_params=_compiler_params(("parallel", "arbitrary"), vmem),
        name="norm_matmul",
    )(h, gain.reshape(1, d), w, tail)


def _matmul_residual_kernel(*refs, n_in):
    a_refs, w_refs = refs[:n_in], refs[n_in:2 * n_in]
    r_ref, o_ref = refs[2 * n_in], refs[2 * n_in + 1]
    acc = r_ref[...]
    for a_ref, w_ref in zip(a_refs, w_refs):
        acc = acc + _dot(a_ref[...], w_ref[...])
    o_ref[...] = acc


def _matmul_residual(acts, w, res, tm=512):
    t, d = res.shape
    n_in = len(acts)
    k = acts[0].shape[1]
    assert all(a.shape[1] == k for a in acts) and w.shape == (n_in * k, d)
    vmem = 2 * 2 * tm * d * 4 + n_in * (2 * tm * k * 2 + 2 * k * d * 2)
    in_specs = [pl.BlockSpec((tm, k), lambda i: (i, 0)) for _ in acts]
    in_specs += [pl.BlockSpec((k, d), functools.partial(lambda i, idx: (idx, 0), idx=idx)) for idx in range(n_in)]
    weights = [w] * n_in
    in_specs.append(pl.BlockSpec((tm, d), lambda i: (i, 0)))
    return pl.pallas_call(
        functools.partial(_matmul_residual_kernel, n_in=n_in),
        grid=(t // tm,),
        in_specs=in_specs,
        out_specs=pl.BlockSpec((tm, d), lambda i: (i, 0)),
        out_shape=jax.ShapeDtypeStruct((t, d), F32),
        compiler_params=_compiler_params(("parallel",), vmem),
        name="matmul_residual",
    )(*acts, *weights, res)


def _gla_kernel(q_ref, k_ref, v_ref, r_ref, glr_ref, w2_ref, gb_ref, gn_ref, o_ref, st_ref, *, n_chunks):
    @pl.when(pl.program_id(1) == 0)
    def _():
        st_ref[...] = jnp.zeros_like(st_ref)

    c = GLA_CHUNK
    lb = n_chunks * c
    ri = lax.broadcasted_iota(jnp.int32, (lb, lb), 0)
    ci = lax.broadcasted_iota(jnp.int32, (lb, lb), 1)
    same_chunk_tril = (ri >= ci) & ((ri // c) == (ci // c))
    tril = jnp.where(same_chunk_tril, 1.0, 0.0).astype(BF16)
    causal = lax.broadcasted_iota(jnp.int32, (c, c), 0) >= lax.broadcasted_iota(jnp.int32, (c, c), 1)
    gn = gn_ref[...]

    z = _dot(glr_ref[...], w2_ref[...]) + gb_ref[...]
    log_a = (jnp.minimum(z, 0.0) - jnp.log1p(jnp.exp(-jnp.abs(z)))) * (1.0 / GLA_TAU)
    bcum = _select_dot_lhs(tril, log_a, 3)
    q_all = q_ref[...].astype(F32) * (GLA_DK ** -0.5) * jnp.exp(bcum)
    k_all = k_ref[...].astype(F32)
    k_in_all = k_all * jnp.exp(-bcum)

    items = [(h, idx) for h in range(GLA_HEADS) for idx in range(n_chunks)]
    kcols = lambda h: slice(h * GLA_DK, (h + 1) * GLA_DK)
    vcols = lambda h: slice(h * GLA_DV, (h + 1) * GLA_DV)
    rows = lambda idx: slice(idx * c, (idx + 1) * c)
    q_in, k_in, k_end, decay_end, v, v_t = {}, {}, {}, {}, {}, {}
    for h, idx in items:
        b_c = bcum[rows(idx), kcols(h)]
        b_last = b_c[c - 1:c, :]
        q_in[h, idx] = q_all[rows(idx), kcols(h)].astype(BF16)
        k_in[h, idx] = k_in_all[rows(idx), kcols(h)].astype(BF16)
        k_end[h, idx] = (k_all[rows(idx), kcols(h)] * jnp.exp(b_last - b_c)).astype(BF16)
        decay_end[h, idx] = jnp.exp(b_last)
        v[h, idx] = v_ref[rows(idx), vcols(h)]
        v_t[h, idx] = v[h, idx].astype(F32).T.astype(BF16)
    attn = {it: jnp.where(causal, _dot_nt(q_in[it], k_in[it]), 0.0).astype(BF16) for it in items}
    o_intra = {it: _dot(attn[it], v[it]) for it in items}
    upd = {it: _dot(v_t[it], k_end[it]) for it in items}
    for h in range(GLA_HEADS):
        state = st_ref[h]
        for idx in range(n_chunks):
            o = o_intra[h, idx] + _dot_nt(q_in[h, idx], state.astype(BF16))
            state = decay_end[h, idx] * state + upd[h, idx]
            o = _rms_norm(o, gn) * _silu(r_ref[rows(idx), vcols(h)].astype(F32))
            o_ref[rows(idx), vcols(h)] = o.astype(o_ref.dtype)
        st_ref[h] = state


def _gla(proj, w2p, gate_b, gla_norm, batch, seq, lb=256):
    t = batch * seq
    ns = seq // lb
    qk_w, v_w = GLA_HEADS * GLA_DK, GLA_HEADS * GLA_DV
    row = lambda b, n: b * ns + n
    vmem = 2 * lb * (2 * qk_w + 2 * v_w + LANES) * 2 + 2 * lb * v_w * 2 + 8 * lb * qk_w * 4 + (4 << 20)
    return pl.pallas_call(
        functools.partial(_gla_kernel, n_chunks=lb // GLA_CHUNK),
        grid=(batch, ns),
        in_specs=[
            pl.BlockSpec((lb, qk_w), lambda b, n: (row(b, n), HYB_Q // qk_w)),
            pl.BlockSpec((lb, qk_w), lambda b, n: (row(b, n), HYB_K // qk_w)),
            pl.BlockSpec((lb, v_w), lambda b, n: (row(b, n), HYB_V // v_w)),
            pl.BlockSpec((lb, v_w), lambda b, n: (row(b, n), HYB_R // v_w)),
            pl.BlockSpec((lb, LANES), lambda b, n: (row(b, n), HYB_GLR // LANES)),
            pl.BlockSpec((LANES, qk_w), lambda b, n: (0, 0)),
            pl.BlockSpec((1, qk_w), lambda b, n: (0, 0)),
            pl.BlockSpec((1, GLA_DV), lambda b, n: (0, 0)),
        ],
        out_specs=pl.BlockSpec((lb, v_w), lambda b, n: (row(b, n), 0)),
        out_shape=jax.ShapeDtypeStruct((t, v_w), BF16),
        scratch_shapes=[pltpu.VMEM((GLA_HEADS, GLA_DV, GLA_DK), F32)],
        compiler_params=_compiler_params(("parallel", "arbitrary"), vmem),
        name="gla",
    )(proj, proj, proj, proj, proj, w2p, gate_b.reshape(1, -1), gla_norm.reshape(1, -1))


def _swa_kernel(sink_ref, pos_ref, q_ref, kv_ref, qn_ref, kn_ref, freq_ref, bd_ref, o_ref, kprev_ref, vprev_ref):
    n = pl.program_id(1)
    w = SWA_BLOCK
    lane = lax.broadcasted_iota(jnp.int32, (w, LANES), 1)
    lane_lo = lane < SWA_HEAD_DIM
    lane_lo_kv = lax.broadcasted_iota(jnp.int32, (2 * w, LANES), 1) < SWA_HEAD_DIM
    dim = lane & (SWA_HEAD_DIM - 1)
    half = ROPE_DIM // 2
    bd = bd_ref[...]

    @pl.when(n == 0)
    def _():
        kprev_ref[...] = jnp.zeros_like(kprev_ref)
        vprev_ref[...] = jnp.zeros_like(vprev_ref)

    ang = pos_ref[...].astype(F32) * freq_ref[...]
    cos, sin = jnp.cos(ang), jnp.sin(ang)
    s_hi = jnp.where((dim >= half) & (dim < ROPE_DIM), sin, 0.0)
    s_lo = jnp.where(dim < half, -sin, 0.0)

    def norm_rope(x, gain):
        ms = _dot((x * x).astype(BF16), bd) * (1.0 / SWA_HEAD_DIM)
        xn = x * lax.rsqrt(ms + NORM_EPS) * gain
        return xn * cos + pltpu.roll(xn, half, 1) * s_hi + pltpu.roll(xn, LANES - half, 1) * s_lo

    k_cur = norm_rope(kv_ref[:, 0:LANES].astype(F32), kn_ref[...])
    v_cur = kv_ref[:, LANES:2 * LANES].astype(F32)
    kk = jnp.concatenate([kprev_ref[...], k_cur], axis=0)
    vv = jnp.concatenate([vprev_ref[...], v_cur], axis=0)
    kprev_ref[...] = k_cur
    vprev_ref[...] = v_cur
    kk_sw = pltpu.roll(kk, SWA_HEAD_DIM, 1)
    vv_sw = pltpu.roll(vv, SWA_HEAD_DIM, 1)
    k2 = [jnp.where(lane_lo_kv, kk, kk_sw).astype(BF16), jnp.where(lane_lo_kv, kk_sw, kk).astype(BF16)]
    v2 = [jnp.where(lane_lo_kv, vv, vv_sw).astype(BF16), jnp.where(lane_lo_kv, vv_sw, vv).astype(BF16)]

    qi = lax.broadcasted_iota(jnp.int32, (w, 2 * w), 0)
    ki = lax.broadcasted_iota(jnp.int32, (w, 2 * w), 1)
    rel = qi + w - ki
    valid = (rel >= 0) & (rel < SWA_WINDOW) & ((ki >= w) | (n > 0))

    qn = qn_ref[...] * (SWA_HEAD_DIM ** -0.5)
    heads_per_kv = SWA_Q_HEADS // SWA_KV_HEADS
    pairs = range(SWA_Q_HEADS // 2)
    heads = [(pair, hf) for pair in pairs for hf in range(2)]
    qp = [norm_rope(q_ref[:, pl.ds(pair * LANES, LANES)].astype(F32), qn) for pair in pairs]
    scores = []
    for pair, hf in heads:
        keep = lane_lo if hf == 0 else jnp.logical_not(lane_lo)
        qm = jnp.where(keep, qp[pair], 0.0).astype(BF16)
        scores.append(jnp.where(valid, _dot_nt(qm, k2[(2 * pair) // heads_per_kv]), -1e30))
    probs, rdenom = [], []
    for (pair, hf), s in zip(heads, scores):
        sink = sink_ref[2 * pair + hf]
        m = jnp.maximum(jnp.max(s, axis=-1, keepdims=True), sink)
        p = jnp.exp(s - m)
        rdenom.append(1.0 / (jnp.sum(p, axis=-1, keepdims=True) + jnp.exp(sink - m)))
        probs.append(p.astype(BF16))
    outs = [_dot(p, v2[(2 * pair) // heads_per_kv]) * r for (pair, hf), p, r in zip(heads, probs, rdenom)]
    for pair in pairs:
        o_ref[:, pl.ds(pair * LANES, LANES)] = jnp.where(lane_lo, outs[2 * pair], outs[2 * pair + 1]).astype(o_ref.dtype)


def _swa(proj, pos_col, sinks, q_norm, k_norm, batch, seq):
    t = batch * seq
    w = SWA_BLOCK
    nb = seq // w
    half = ROPE_DIM // 2
    inv_freq = ROPE_THETA ** (-2.0 * jnp.arange(half, dtype=F32) / ROPE_DIM)
    dim = np.arange(LANES) % SWA_HEAD_DIM
    freq_row = jnp.where(dim < ROPE_DIM, inv_freq[dim % half], 0.0).reshape(1, LANES).astype(F32)
    blockdiag = jnp.asarray(np.kron(np.eye(LANES // SWA_HEAD_DIM), np.ones((SWA_HEAD_DIM, SWA_HEAD_DIM))), BF16)
    cur = lambda b, n: b * nb + n
    q_w = SWA_Q_HEADS * SWA_HEAD_DIM
    vmem = 2 * w * (q_w + 2 * LANES) * 2 + 2 * w * q_w * 2 + 4 * w * LANES * 4 + (8 << 20)
    return pl.pallas_call(
        _swa_kernel,
        grid=(batch, nb),
        in_specs=[
            pl.BlockSpec(memory_space=pltpu.SMEM),
            pl.BlockSpec((w, 1), lambda b, n: (cur(b, n), 0)),
            pl.BlockSpec((w, q_w), lambda b, n: (cur(b, n), HYB_SQ // q_w)),
            pl.BlockSpec((w, 2 * LANES), lambda b, n: (cur(b, n), HYB_SK // (2 * LANES))),
            pl.BlockSpec((1, LANES), lambda b, n: (0, 0)),
            pl.BlockSpec((1, LANES), lambda b, n: (0, 0)),
            pl.BlockSpec((1, LANES), lambda b, n: (0, 0)),
            pl.BlockSpec((LANES, LANES), lambda b, n: (0, 0)),
        ],
        out_specs=pl.BlockSpec((w, q_w), lambda b, n: (cur(b, n), 0)),
        out_shape=jax.ShapeDtypeStruct((t, q_w), BF16),
        scratch_shapes=[pltpu.VMEM((w, LANES), F32), pltpu.VMEM((w, LANES), F32)],
        compiler_params=_compiler_params(("parallel", "arbitrary"), vmem),
        name="swa",
    )(sinks.astype(F32), pos_col, proj, proj,
      jnp.tile(q_norm, 2).reshape(1, LANES), jnp.tile(k_norm, 2).reshape(1, LANES), freq_row, blockdiag)


def _ssd_kernel(x_ref, b_ref, c_ref, z_ref, dt_ref, cw_ref, cb_ref, dtb_ref, alog_ref, dsk_ref, ng_ref, e_ref, bm_ref,
                o_ref, tail_ref, st_ref, *, n_chunks):
    n = pl.program_id(1)
    L = SSD_CHUNK
    lb = n_chunks * L
    gw = SSD_GROUP_W
    hp = SSD_HEAD_DIM
    ns = SSD_D_STATE
    groups = range(SSD_GROUPS)
    chunks = range(n_chunks)
    crow = lambda c: slice(c * L, (c + 1) * L)

    @pl.when(n == 0)
    def _():
        tail_ref[...] = jnp.zeros_like(tail_ref)
        st_ref[...] = jnp.zeros_like(st_ref)

    cur = jnp.concatenate([x_ref[...], b_ref[...], c_ref[...]], axis=1).astype(F32)
    ext = jnp.concatenate([tail_ref[...], cur], axis=0)
    tail_ref[...] = cur[lb - SUBLANES:lb, :]
    acc = cb_ref[...] + cw_ref[SSD_CONV - 1:SSD_CONV, :] * cur
    for j in range(SSD_CONV - 1):
        shifted = pltpu.roll(ext, SSD_CONV - 1 - j, 0)[SUBLANES:SUBLANES + lb, :]
        acc = acc + cw_ref[j:j + 1, :] * shifted
    xbc = _silu(acc)

    lane = lax.broadcasted_iota(jnp.int32, (1, LANES), 1)
    dt = _softplus(dt_ref[...].astype(F32) + dtb_ref[...])
    a_neg = jnp.where(lane < SSD_HEADS, -jnp.exp(alog_ref[...]), 0.0)
    ri = lax.broadcasted_iota(jnp.int32, (lb, lb), 0)
    ci = lax.broadcasted_iota(jnp.int32, (lb, lb), 1)
    tril = jnp.where((ri >= ci) & ((ri // L) == (ci // L)), 1.0, 0.0).astype(BF16)
    acum = _select_dot_lhs(tril, dt * a_neg, 3)
    wgt_small = jnp.concatenate([jnp.exp(acum[c * L + L - 1:c * L + L, :] - acum[crow(c), :]) for c in chunks], axis=0)
    acum_cat = jnp.concatenate(_split_bf16(acum, 3), axis=1)
    dt_cat = jnp.concatenate(_split_bf16(dt, 2), axis=1)
    eacum_cat = jnp.concatenate(_split_bf16(jnp.exp(acum), 2), axis=1)
    wgt_cat = jnp.concatenate(_split_bf16(wgt_small, 2), axis=1)
    acum_t = [jnp.concatenate([acum[crow(c), :], acum[crow(c), :]], axis=0).T for c in chunks]

    lane_lo = lane < hp
    row_l = lax.broadcasted_iota(jnp.int32, (L, gw), 0)
    src_s = lax.broadcasted_iota(jnp.int32, (L, gw), 1) & (hp - 1)
    causal = src_s <= row_l
    blockmask = bm_ref[...]
    heads_per_group = SSD_HEADS // SSD_GROUPS

    xs, b_bf, c_bf = [], [], []
    for g in groups:
        xs.append(xbc[:, g * gw:(g + 1) * gw])
        b_bf.append(xbc[:, SSD_D_INNER + g * ns:SSD_D_INNER + (g + 1) * ns])
        c_bf.append(xbc[:, SSD_D_INNER + SSD_BC_W + g * ns:SSD_D_INNER + SSD_BC_W + (g + 1) * ns].astype(BF16))
    b_t = [[b_bf[g][crow(c), :].T.astype(BF16) for g in groups] for c in chunks]
    b_bf = [b.astype(BF16) for b in b_bf]

    col, dt_exp, ecol, wgt = [], [], [], []
    for g in groups:
        e_g = e_ref[:, pl.ds(g * gw, gw)]
        col.append(_dot(acum_cat, e_g))
        dt_exp.append(_dot(dt_cat, e_g[:2 * LANES, :]))
        ecol.append(_dot(eacum_cat, e_g[:2 * LANES, :]))
        wgt.append(_dot(wgt_cat, e_g[:2 * LANES, :]))
    cb = [[_dot_nt(c_bf[g][crow(c), :], b_bf[g][crow(c), :]) for g in groups] for c in chunks]

    xdt = [xs[g] * dt_exp[g] for g in groups]
    xdt_bf = [t.astype(BF16) for t in xdt]
    xw = [(xdt[g] * wgt[g]).astype(BF16) for g in groups]
    m = []
    for c in chunks:
        m_c = []
        for g in groups:
            row_parts = []
            for pr in range(heads_per_group // 2):
                h0 = g * heads_per_group + 2 * pr
                r = jnp.where(lane_lo, acum_t[c][h0:h0 + 1, :], acum_t[c][h0 + 1:h0 + 2, :])
                row_parts.append(jnp.broadcast_to(r, (L, LANES)))
            row = jnp.concatenate(row_parts, axis=1)
            decay = jnp.where(causal, jnp.exp(col[g][crow(c), :] - row), 0.0)
            cb2 = jnp.concatenate([cb[c][g], cb[c][g]], axis=1)
            cb_exp = jnp.concatenate([cb2] * (gw // LANES), axis=1)
            m_c.append((cb_exp * decay).astype(BF16))
        m.append(m_c)

    y_diag, upd = [], []
    for c in chunks:
        yd_c, upd_c = [], []
        for g in groups:
            y_parts = []
            for qd in range(gw // MXU_TILE):
                sl = slice(qd * MXU_TILE, (qd + 1) * MXU_TILE)
                rhs = jnp.concatenate([xdt_bf[g][crow(c), sl]] * (MXU_TILE // hp), axis=0) * blockmask
                y_parts.append(_dot(m[c][g][:, sl], rhs))
            yd_c.append(jnp.concatenate(y_parts, axis=1))
            upd_c.append(_dot(b_t[c][g], xw[g][crow(c), :]))
        y_diag.append(yd_c)
        upd.append(upd_c)

    for g in groups:
        cols = pl.ds(g * gw, gw)
        state = st_ref[g]
        y_g = []
        for c in chunks:
            y_off = _dot(c_bf[g][crow(c), :], state.astype(BF16)) * ecol[g][crow(c), :]
            y_g.append(y_diag[c][g] + y_off)
            state = ecol[g][c * L + L - 1:c * L + L, :] * state + upd[c][g]
        st_ref[g] = state
        y = jnp.concatenate(y_g, axis=0) + xs[g] * dsk_ref[:, cols]
        y = y * _silu(z_ref[:, cols].astype(F32))
        o_ref[:, cols] = _rms_norm(y, ng_ref[:, cols]).astype(o_ref.dtype)


def _ssd(proj, conv_w, conv_b, dt_bias, a_log, d_skip, norm_g, batch, seq, n_chunks=2):
    t = batch * seq
    L = n_chunks * SSD_CHUNK
    ns = seq // L
    pad = LANES - SSD_HEADS
    expand = np.kron(np.eye(LANES, SSD_HEADS), np.ones((1, SSD_HEAD_DIM)))
    expand = jnp.asarray(np.tile(expand, (3, 1)), BF16)
    blockmask = jnp.asarray(np.kron(np.eye(MXU_TILE // SSD_HEAD_DIM), np.ones((SSD_HEAD_DIM, SSD_HEAD_DIM))), BF16)
    row = lambda b, n: b * ns + n
    const = lambda b, n: (0, 0)
    vmem = (2 * L * (SSD_CONV_CH + SSD_D_INNER + LANES) * 2 + 2 * L * SSD_D_INNER * 2
            + 2 * (6 * SSD_CONV_CH * 4 + 3 * LANES * SSD_D_INNER * 2)
            + SUBLANES * SSD_CONV_CH * 4 + SSD_GROUPS * SSD_D_STATE * SSD_GROUP_W * 4 + (12 << 20))
    return pl.pallas_call(
        functools.partial(_ssd_kernel, n_chunks=n_chunks),
        grid=(batch, ns),
        in_specs=[
            pl.BlockSpec((L, SSD_D_INNER), lambda b, n: (row(b, n), SSD_X // SSD_D_INNER)),
            pl.BlockSpec((L, SSD_BC_W), lambda b, n: (row(b, n), SSD_B // SSD_BC_W)),
            pl.BlockSpec((L, SSD_BC_W), lambda b, n: (row(b, n), SSD_C // SSD_BC_W)),
            pl.BlockSpec((L, SSD_D_INNER), lambda b, n: (row(b, n), SSD_Z // SSD_D_INNER)),
            pl.BlockSpec((L, LANES), lambda b, n: (row(b, n), SSD_DT // LANES)),
            pl.BlockSpec((SSD_CONV, SSD_CONV_CH), const),
            pl.BlockSpec((1, SSD_CONV_CH), const),
            pl.BlockSpec((1, LANES), const),
            pl.BlockSpec((1, LANES), const),
            pl.BlockSpec((1, SSD_D_INNER), const),
            pl.BlockSpec((1, SSD_D_INNER), const),
            pl.BlockSpec((3 * LANES, SSD_D_INNER), const),
            pl.BlockSpec((MXU_TILE, MXU_TILE), const),
        ],
        out_specs=pl.BlockSpec((L, SSD_D_INNER), lambda b, n: (row(b, n), 0)),
        out_shape=jax.ShapeDtypeStruct((t, SSD_D_INNER), BF16),
        scratch_shapes=[pltpu.VMEM((SUBLANES, SSD_CONV_CH), F32),
                        pltpu.VMEM((SSD_GROUPS, SSD_D_STATE, SSD_GROUP_W), F32)],
        compiler_params=_compiler_params(("parallel", "arbitrary"), vmem),
        name="ssd",
    )(proj, proj, proj, proj, proj, conv_w, conv_b.reshape(1, -1),
      jnp.pad(dt_bias, (0, pad)).reshape(1, LANES), jnp.pad(a_log, (0, pad)).reshape(1, LANES),
      jnp.repeat(d_skip, SSD_HEAD_DIM).reshape(1, -1), norm_g.reshape(1, -1), expand, blockmask)


def _hyb_weight(w_in):
    w = w_in.astype(BF16)
    glr_end = HYB_SQ + GLA_GATE_RANK
    zeros = jnp.zeros((w.shape[0], HYB_W - HYB_GLR - GLA_GATE_RANK), BF16)
    return jnp.concatenate([w[:, :HYB_SQ], w[:, glr_end:], w[:, HYB_SQ:glr_end], zeros], axis=1)


def _ssd_weight(w_in):
    return w_in.astype(BF16)


def _gla_swa_mixer(h, pos_col, norm_gain, w_in, gate_w2, gate_b, gla_norm, q_norm, k_norm, sinks, w_out, batch, seq):
    proj = _norm_matmul(h, norm_gain, _hyb_weight(w_in), tn=1536)
    w2p = jnp.pad(gate_w2, ((0, LANES - GLA_GATE_RANK), (0, 0))).astype(BF16)
    o_gla = _gla(proj, w2p, gate_b, gla_norm, batch, seq)
    o_swa = _swa(proj, pos_col, sinks, q_norm, k_norm, batch, seq)
    return _matmul_residual([o_gla, o_swa], w_out.astype(BF16), h)


def _ssd_mixer(h, norm_gain, w_in, conv_w, conv_b, dt_bias, a_log, d_skip, norm_g, w_out, batch, seq):
    proj = _norm_matmul(h, norm_gain, _ssd_weight(w_in), tn=1536)
    y = _ssd(proj, conv_w, conv_b, dt_bias, a_log, d_skip, norm_g, batch, seq)
    return _matmul_residual([y], w_out.astype(BF16), h)


def kernel(x, positions, norm_ffn, w_ffn_gu, w_ffn_down, norm_mix, hyb_w_in, gla_gate_w2, gla_gate_b, gla_norm,
           attn_q_norm, attn_k_norm, attn_sinks, hyb_w_out, ssd_w_in, ssd_conv_w, ssd_conv_b, ssd_dt_bias, ssd_a_log,
           ssd_d, ssd_norm, ssd_w_out):
    batch, seq, d = x.shape
    h = x.reshape(batch * seq, d)
    pos_col = positions.reshape(batch * seq, 1)
    depth = norm_ffn.shape[0]
    ffns = [(layer, pos) for layer in range(depth) for pos in range(2)]
    weights = (w_ffn_gu[0, 0].astype(BF16), w_ffn_down[0, 0].astype(BF16))
    for layer in range(depth):
        i = layer // 2
        for pos in range(2):
            if pos == 1:
                if layer % 2 == 0:
                    h = _gla_swa_mixer(h, pos_col, norm_mix[layer], hyb_w_in[i], gla_gate_w2[i], gla_gate_b[i],
                                       gla_norm[i], attn_q_norm[i], attn_k_norm[i], attn_sinks[i], hyb_w_out[i],
                                       batch, seq)
                else:
                    h = _ssd_mixer(h, norm_mix[layer], ssd_w_in[i], ssd_conv_w[i], ssd_conv_b[i], ssd_dt_bias[i],
                                   ssd_a_log[i], ssd_d[i], ssd_norm[i], ssd_w_out[i], batch, seq)
            k = ffns.index((layer, pos))
            nxt = (w_ffn_gu, w_ffn_down) + ffns[k + 1] if k + 1 < len(ffns) else None
            h, weights = _ffn(h, norm_ffn[layer, pos], weights[0], weights[1], nxt)
    return h.reshape(batch, seq, d)
```

```python
import functools

import numpy as np
import jax
import jax.numpy as jnp
from jax import lax
from jax.experimental import pallas as pl
from jax.experimental.pallas import tpu as pltpu

F32 = jnp.float32
BF16 = jnp.bfloat16

D_MODEL = 2048
D_FF = 5632
NORM_EPS = 1e-6
MACARON_WEIGHT = 0.5

GLA_HEADS = 4
GLA_DK = 128
GLA_DV = 256
GLA_GATE_RANK = 16
GLA_TAU = 16.0
GLA_CHUNK = 64

SWA_HEAD_DIM = 64
SWA_Q_HEADS = 16
SWA_KV_HEADS = 2
SWA_WINDOW = 128
SWA_BLOCK = 128
ROPE_THETA = 500000.0
ROPE_DIM = 16

SSD_D_INNER = 4096
SSD_HEAD_DIM = 64
SSD_HEADS = 64
SSD_GROUPS = 8
SSD_D_STATE = 128
SSD_CONV = 4
SSD_CHUNK = 64
SSD_GROUP_W = SSD_D_INNER // SSD_GROUPS
SSD_BC_W = SSD_GROUPS * SSD_D_STATE
SSD_CONV_CH = SSD_D_INNER + 2 * SSD_BC_W

LANES = 128
SUBLANES = 8
MXU_TILE = 256
VMEM_LIMIT_CAP = 60 * 1024 * 1024

HYB_Q, HYB_K, HYB_V, HYB_R, HYB_SQ, HYB_SK, HYB_SV, HYB_GLR = 0, 512, 1024, 2048, 3072, 4096, 4224, 4352
HYB_W = 4608
SSD_Z, SSD_X, SSD_B, SSD_C, SSD_DT = 0, 4096, 8192, 9216, 10240
SSD_W = 10752


def _compiler_params(semantics, vmem_bytes):
    limit = min(int(vmem_bytes * 1.25) + (4 << 20), VMEM_LIMIT_CAP)
    return pltpu.CompilerParams(dimension_semantics=semantics, vmem_limit_bytes=limit)


def _rms_norm(x, gain):
    ms = jnp.mean(x * x, axis=-1, keepdims=True)
    return x * lax.rsqrt(ms + NORM_EPS) * gain


def _scaled_rows(x, gain, hn_ref, rs_ref):
    ms = jnp.mean(x * x, axis=-1, keepdims=True)
    hn_ref[...] = (x * gain).astype(BF16)
    rs_ref[...] = jnp.broadcast_to(lax.rsqrt(ms + NORM_EPS), rs_ref.shape)


def _silu(x):
    return x * jax.nn.sigmoid(x)


def _softplus(x):
    return jnp.maximum(x, 0.0) + jnp.log1p(jnp.exp(-jnp.abs(x)))


def _split_bf16(x, n):
    parts, rest = [], x
    for _ in range(n):
        p = rest.astype(BF16)
        parts.append(p)
        rest = rest - p.astype(F32)
    return parts


def _dot(a, b):
    return jnp.dot(a, b, preferred_element_type=F32)


def _dot_nt(a, b):
    return lax.dot_general(a, b, (((1,), (1,)), ((), ())), preferred_element_type=F32)


def _select_dot_lhs(sel, x, n):
    return sum(_dot(sel, p) for p in _split_bf16(x, n))


def _ffn_kernel(x_ref, g_ref, wg_ref, wu_ref, wd_ref, *rest, cast_next):
    if cast_next:
        ngu_ref, nd_ref, o_ref, ogu_ref, od_ref, hn_ref, rs_ref = rest
    else:
        o_ref, hn_ref, rs_ref = rest

    @pl.when(pl.program_id(1) == 0)
    def _():
        x = x_ref[...]
        _scaled_rows(x, g_ref[...], hn_ref, rs_ref)
        o_ref[...] = x

    h = hn_ref[...]
    rs = jnp.concatenate([rs_ref[...]] * (wg_ref.shape[1] // LANES), axis=1)
    gate = _dot(h, wg_ref[...]) * rs
    up = _dot(h, wu_ref[...]) * (rs * MACARON_WEIGHT)
    act = (_silu(gate) * up).astype(BF16)
    o_ref[...] += _dot(act, wd_ref[...])
    if cast_next:
        ogu_ref[...] = ngu_ref[...].astype(BF16)
        od_ref[...] = nd_ref[...].astype(BF16)


def _ffn(h, gain, w_gu, w_down, next_weights=None, tm=1024, tf=512):
    t, d = h.shape
    nf = D_FF // tf
    ni = t // tm
    vmem = 2 * (2 * tm * d * 4) + 2 * 3 * d * tf * 2 + tm * d * 2 + 3 * tm * tf * 4
    in_specs = [
        pl.BlockSpec((tm, d), lambda i, j: (i, 0)),
        pl.BlockSpec((1, d), lambda i, j: (0, 0)),
        pl.BlockSpec((d, tf), lambda i, j: (0, j)),
        pl.BlockSpec((d, tf), lambda i, j: (0, j + nf)),
        pl.BlockSpec((tf, d), lambda i, j: (j, 0)),
    ]
    out_specs = [pl.BlockSpec((tm, d), lambda i, j: (i, 0))]
    out_shape = [jax.ShapeDtypeStruct((t, d), F32)]
    args = [h, gain.reshape(1, d), w_gu, w_gu, w_down]
    if next_weights is not None:
        w_gu_all, w_down_all, layer, pos = next_weights
        gu_blk = (d // ni, 2 * D_FF // nf)
        dn_blk = (D_FF // nf, d // ni)
        in_specs += [
            pl.BlockSpec((None, None) + gu_blk, lambda i, j: (layer, pos, i, j)),
            pl.BlockSpec((None, None) + dn_blk, lambda i, j: (layer, pos, j, i)),
        ]
        out_specs += [pl.BlockSpec(gu_blk, lambda i, j: (i, j)), pl.BlockSpec(dn_blk, lambda i, j: (j, i))]
        out_shape += [jax.ShapeDtypeStruct((d, 2 * D_FF), BF16), jax.ShapeDtypeStruct((D_FF, d), BF16)]
        args += [w_gu_all, w_down_all]
        vmem += 2 * (gu_blk[0] * gu_blk[1] + dn_blk[0] * dn_blk[1]) * (4 + 2)
    outs = pl.pallas_call(
        functools.partial(_ffn_kernel, cast_next=next_weights is not None),
        grid=(ni, nf),
        in_specs=in_specs,
        out_specs=out_specs,
        out_shape=out_shape,
        scratch_shapes=[pltpu.VMEM((tm, d), BF16), pltpu.VMEM((tm, LANES), F32)],
        compiler_params=_compiler_params(("parallel", "arbitrary"), vmem),
        name="ffn",
    )(*args)
    return outs[0], (tuple(outs[1:]) if next_weights is not None else None)


def _norm_matmul_kernel(x_ref, g_ref, w_ref, o_ref, hn_ref):
    @pl.when(pl.program_id(1) == 0)
    def _():
        hn_ref[...] = _rms_norm(x_ref[...], g_ref[...]).astype(BF16)

    o_ref[...] = _dot(hn_ref[...], w_ref[...]).astype(o_ref.dtype)


def _norm_matmul(h, gain, w, tn, tm=1024):
    t, d = h.shape
    n = w.shape[1]
    vmem = 2 * tm * d * 4 + 2 * d * tn * 2 + 2 * tm * tn * 2 + tm * d * 2 + tm * tn * 4
    return pl.pallas_call(
        _norm_matmul_kernel,
        grid=(t // tm, n // tn),
        in_specs=[
            pl.BlockSpec((tm, d), lambda i, j: (i, 0)),
            pl.BlockSpec((1, d), lambda i, j: (0, 0)),
            pl.BlockSpec((d, tn), lambda i, j: (0, j)),
        ],
        out_specs=pl.BlockSpec((tm, tn), lambda i, j: (i, j)),
        out_shape=jax.ShapeDtypeStruct((t, n), BF16),
        scratch_shapes=[pltpu.VMEM((tm, d), BF16)],
        compiler_params=_compiler_params(("parallel", "arbitrary"), vmem),
        name="norm_matmul",
    )(h, gain.reshape(1, d), w)


def _matmul_residual_kernel(*refs, n_in):
    a_refs, w_refs = refs[:n_in], refs[n_in:2 * n_in]
    r_ref, o_ref = refs[2 * n_in], refs[2 * n_in + 1]
    acc = r_ref[...]
    for a_ref, w_ref in zip(a_refs, w_refs):
        acc = acc + _dot(a_ref[...], w_ref[...])
    o_ref[...] = acc


def _matmul_residual(acts, w, res, tm=512):
    t, d = res.shape
    n_in = len(acts)
    k = acts[0].shape[1]
    assert all(a.shape[1] == k for a in acts) and w.shape == (n_in * k, d)
    vmem = 2 * 2 * tm * d * 4 + n_in * (2 * tm * k * 2 + 2 * k * d * 2)
    in_specs = [pl.BlockSpec((tm, k), lambda i: (i, 0)) for _ in acts]
    in_specs += [pl.BlockSpec((k, d), functools.partial(lambda i, idx: (idx, 0), idx=idx)) for idx in range(n_in)]
    weights = [w] * n_in
    in_specs.append(pl.BlockSpec((tm, d), lambda i: (i, 0)))
    return pl.pallas_call(
        functools.partial(_matmul_residual_kernel, n_in=n_in),
        grid=(t // tm,),
        in_specs=in_specs,
        out_specs=pl.BlockSpec((tm, d), lambda i: (i, 0)),
        out_shape=jax.ShapeDtypeStruct((t, d), F32),
        compiler_params=_compiler_params(("parallel",), vmem),
        name="matmul_residual",
    )(*acts, *weights, res)


def _gla_kernel(q_ref, k_ref, v_ref, r_ref, glr_ref, w2_ref, gb_ref, gn_ref, o_ref, st_ref, *, n_chunks):
    @pl.when(pl.program_id(1) == 0)
    def _():
        st_ref[...] = jnp.zeros_like(st_ref)

    c = GLA_CHUNK
    lb = n_chunks * c
    ri = lax.broadcasted_iota(jnp.int32, (lb, lb), 0)
    ci = lax.broadcasted_iota(jnp.int32, (lb, lb), 1)
    same_chunk_tril = (ri >= ci) & ((ri // c) == (ci // c))
    tril = jnp.where(same_chunk_tril, 1.0, 0.0).astype(BF16)
    causal = lax.broadcasted_iota(jnp.int32, (c, c), 0) >= lax.broadcasted_iota(jnp.int32, (c, c), 1)
    gn = gn_ref[...]

    z = _dot(glr_ref[...], w2_ref[...]) + gb_ref[...]
    log_a = (jnp.minimum(z, 0.0) - jnp.log1p(jnp.exp(-jnp.abs(z)))) * (1.0 / GLA_TAU)
    bcum = _select_dot_lhs(tril, log_a, 3)
    q_all = q_ref[...].astype(F32) * (GLA_DK ** -0.5) * jnp.exp(bcum)
    k_all = k_ref[...].astype(F32)
    k_in_all = k_all * jnp.exp(-bcum)

    items = [(h, idx) for h in range(GLA_HEADS) for idx in range(n_chunks)]
    kcols = lambda h: slice(h * GLA_DK, (h + 1) * GLA_DK)
    vcols = lambda h: slice(h * GLA_DV, (h + 1) * GLA_DV)
    rows = lambda idx: slice(idx * c, (idx + 1) * c)
    q_in, k_in, k_end, decay_end, v, v_t = {}, {}, {}, {}, {}, {}
    for h, idx in items:
        b_c = bcum[rows(idx), kcols(h)]
        b_last = b_c[c - 1:c, :]
        q_in[h, idx] = q_all[rows(idx), kcols(h)].astype(BF16)
        k_in[h, idx] = k_in_all[rows(idx), kcols(h)].astype(BF16)
        k_end[h, idx] = (k_all[rows(idx), kcols(h)] * jnp.exp(b_last - b_c)).astype(BF16)
        decay_end[h, idx] = jnp.exp(b_last)
        v[h, idx] = v_ref[rows(idx), vcols(h)]
        v_t[h, idx] = v[h, idx].astype(F32).T.astype(BF16)
    attn = {it: jnp.where(causal, _dot_nt(q_in[it], k_in[it]), 0.0).astype(BF16) for it in items}
    o_intra = {it: _dot(attn[it], v[it]) for it in items}
    upd = {it: _dot(v_t[it], k_end[it]) for it in items}
    for h in range(GLA_HEADS):
        state = st_ref[h]
        for idx in range(n_chunks):
            o = o_intra[h, idx] + _dot_nt(q_in[h, idx], state.astype(BF16))
            state = decay_end[h, idx] * state + upd[h, idx]
            o = _rms_norm(o, gn) * _silu(r_ref[rows(idx), vcols(h)].astype(F32))
            o_ref[rows(idx), vcols(h)] = o.astype(o_ref.dtype)
        st_ref[h] = state


def _gla(proj, w2p, gate_b, gla_norm, batch, seq, lb=256):
    t = batch * seq
    ns = seq // lb
    qk_w, v_w = GLA_HEADS * GLA_DK, GLA_HEADS * GLA_DV
    row = lambda b, n: b * ns + n
    vmem = 2 * lb * (2 * qk_w + 2 * v_w + LANES) * 2 + 2 * lb * v_w * 2 + 8 * lb * qk_w * 4 + (4 << 20)
    return pl.pallas_call(
        functools.partial(_gla_kernel, n_chunks=lb // GLA_CHUNK),
        grid=(batch, ns),
        in_specs=[
            pl.BlockSpec((lb, qk_w), lambda b, n: (row(b, n), HYB_Q // qk_w)),
            pl.BlockSpec((lb, qk_w), lambda b, n: (row(b, n), HYB_K // qk_w)),
            pl.BlockSpec((lb, v_w), lambda b, n: (row(b, n), HYB_V // v_w)),
            pl.BlockSpec((lb, v_w), lambda b, n: (row(b, n), HYB_R // v_w)),
            pl.BlockSpec((lb, LANES), lambda b, n: (row(b, n), HYB_GLR // LANES)),
            pl.BlockSpec((LANES, qk_w), lambda b, n: (0, 0)),
            pl.BlockSpec((1, qk_w), lambda b, n: (0, 0)),
            pl.BlockSpec((1, GLA_DV), lambda b, n: (0, 0)),
        ],
        out_specs=pl.BlockSpec((lb, v_w), lambda b, n: (row(b, n), 0)),
        out_shape=jax.ShapeDtypeStruct((t, v_w), BF16),
        scratch_shapes=[pltpu.VMEM((GLA_HEADS, GLA_DV, GLA_DK), F32)],
        compiler_params=_compiler_params(("parallel", "arbitrary"), vmem),
        name="gla",
    )(proj, proj, proj, proj, proj, w2p, gate_b.reshape(1, -1), gla_norm.reshape(1, -1))


def _swa_kernel(sink_ref, pos_ref, q_ref, kv_ref, qn_ref, kn_ref, freq_ref, bd_ref, o_ref, kprev_ref, vprev_ref):
    n = pl.program_id(1)
    w = SWA_BLOCK
    lane = lax.broadcasted_iota(jnp.int32, (w, LANES), 1)
    lane_lo = lane < SWA_HEAD_DIM
    lane_lo_kv = lax.broadcasted_iota(jnp.int32, (2 * w, LANES), 1) < SWA_HEAD_DIM
    dim = lane & (SWA_HEAD_DIM - 1)
    half = ROPE_DIM // 2
    bd = bd_ref[...]

    @pl.when(n == 0)
    def _():
        kprev_ref[...] = jnp.zeros_like(kprev_ref)
        vprev_ref[...] = jnp.zeros_like(vprev_ref)

    ang = pos_ref[...].astype(F32) * freq_ref[...]
    cos, sin = jnp.cos(ang), jnp.sin(ang)
    s_hi = jnp.where((dim >= half) & (dim < ROPE_DIM), sin, 0.0)
    s_lo = jnp.where(dim < half, -sin, 0.0)

    def norm_rope(x, gain):
        ms = _dot((x * x).astype(BF16), bd) * (1.0 / SWA_HEAD_DIM)
        xn = x * lax.rsqrt(ms + NORM_EPS) * gain
        return xn * cos + pltpu.roll(xn, half, 1) * s_hi + pltpu.roll(xn, LANES - half, 1) * s_lo

    k_cur = norm_rope(kv_ref[:, 0:LANES].astype(F32), kn_ref[...])
    v_cur = kv_ref[:, LANES:2 * LANES].astype(F32)
    kk = jnp.concatenate([kprev_ref[...], k_cur], axis=0)
    vv = jnp.concatenate([vprev_ref[...], v_cur], axis=0)
    kprev_ref[...] = k_cur
    vprev_ref[...] = v_cur
    kk_sw = pltpu.roll(kk, SWA_HEAD_DIM, 1)
    vv_sw = pltpu.roll(vv, SWA_HEAD_DIM, 1)
    k2 = [jnp.where(lane_lo_kv, kk, kk_sw).astype(BF16), jnp.where(lane_lo_kv, kk_sw, kk).astype(BF16)]
    v2 = [jnp.where(lane_lo_kv, vv, vv_sw).astype(BF16), jnp.where(lane_lo_kv, vv_sw, vv).astype(BF16)]

    qi = lax.broadcasted_iota(jnp.int32, (w, 2 * w), 0)
    ki = lax.broadcasted_iota(jnp.int32, (w, 2 * w), 1)
    rel = qi + w - ki
    valid = (rel >= 0) & (rel < SWA_WINDOW) & ((ki >= w) | (n > 0))

    qn = qn_ref[...] * (SWA_HEAD_DIM ** -0.5)
    heads_per_kv = SWA_Q_HEADS // SWA_KV_HEADS
    pairs = range(SWA_Q_HEADS // 2)
    heads = [(pair, hf) for pair in pairs for hf in range(2)]
    qp = [norm_rope(q_ref[:, pl.ds(pair * LANES, LANES)].astype(F32), qn) for pair in pairs]
    scores = []
    for pair, hf in heads:
        keep = lane_lo if hf == 0 else jnp.logical_not(lane_lo)
        qm = jnp.where(keep, qp[pair], 0.0).astype(BF16)
        scores.append(jnp.where(valid, _dot_nt(qm, k2[(2 * pair) // heads_per_kv]), -1e30))
    probs, rdenom = [], []
    for (pair, hf), s in zip(heads, scores):
        sink = sink_ref[2 * pair + hf]
        m = jnp.maximum(jnp.max(s, axis=-1, keepdims=True), sink)
        p = jnp.exp(s - m)
        rdenom.append(1.0 / (jnp.sum(p, axis=-1, keepdims=True) + jnp.exp(sink - m)))
        probs.append(p.astype(BF16))
    outs = [_dot(p, v2[(2 * pair) // heads_per_kv]) * r for (pair, hf), p, r in zip(heads, probs, rdenom)]
    for pair in pairs:
        o_ref[:, pl.ds(pair * LANES, LANES)] = jnp.where(lane_lo, outs[2 * pair], outs[2 * pair + 1]).astype(o_ref.dtype)


def _swa(proj, pos_col, sinks, q_norm, k_norm, batch, seq):
    t = batch * seq
    w = SWA_BLOCK
    nb = seq // w
    half = ROPE_DIM // 2
    inv_freq = ROPE_THETA ** (-2.0 * jnp.arange(half, dtype=F32) / ROPE_DIM)
    dim = np.arange(LANES) % SWA_HEAD_DIM
    freq_row = jnp.where(dim < ROPE_DIM, inv_freq[dim % half], 0.0).reshape(1, LANES).astype(F32)
    blockdiag = jnp.asarray(np.kron(np.eye(LANES // SWA_HEAD_DIM), np.ones((SWA_HEAD_DIM, SWA_HEAD_DIM))), BF16)
    cur = lambda b, n: b * nb + n
    q_w = SWA_Q_HEADS * SWA_HEAD_DIM
    vmem = 2 * w * (q_w + 2 * LANES) * 2 + 2 * w * q_w * 2 + 4 * w * LANES * 4 + (8 << 20)
    return pl.pallas_call(
        _swa_kernel,
        grid=(batch, nb),
        in_specs=[
            pl.BlockSpec(memory_space=pltpu.SMEM),
            pl.BlockSpec((w, 1), lambda b, n: (cur(b, n), 0)),
            pl.BlockSpec((w, q_w), lambda b, n: (cur(b, n), HYB_SQ // q_w)),
            pl.BlockSpec((w, 2 * LANES), lambda b, n: (cur(b, n), HYB_SK // (2 * LANES))),
            pl.BlockSpec((1, LANES), lambda b, n: (0, 0)),
            pl.BlockSpec((1, LANES), lambda b, n: (0, 0)),
            pl.BlockSpec((1, LANES), lambda b, n: (0, 0)),
            pl.BlockSpec((LANES, LANES), lambda b, n: (0, 0)),
        ],
        out_specs=pl.BlockSpec((w, q_w), lambda b, n: (cur(b, n), 0)),
        out_shape=jax.ShapeDtypeStruct((t, q_w), BF16),
        scratch_shapes=[pltpu.VMEM((w, LANES), F32), pltpu.VMEM((w, LANES), F32)],
        compiler_params=_compiler_params(("parallel", "arbitrary"), vmem),
        name="swa",
    )(sinks.astype(F32), pos_col, proj, proj,
      jnp.tile(q_norm, 2).reshape(1, LANES), jnp.tile(k_norm, 2).reshape(1, LANES), freq_row, blockdiag)


def _ssd_kernel(x_ref, b_ref, c_ref, z_ref, dt_ref, cw_ref, cb_ref, dtb_ref, alog_ref, dsk_ref, ng_ref, e_ref, bm_ref,
                o_ref, tail_ref, st_ref, *, n_chunks):
    n = pl.program_id(1)
    L = SSD_CHUNK
    lb = n_chunks * L
    gw = SSD_GROUP_W
    hp = SSD_HEAD_DIM
    ns = SSD_D_STATE
    groups = range(SSD_GROUPS)
    chunks = range(n_chunks)
    crow = lambda c: slice(c * L, (c + 1) * L)

    @pl.when(n == 0)
    def _():
        tail_ref[...] = jnp.zeros_like(tail_ref)
        st_ref[...] = jnp.zeros_like(st_ref)

    cur = jnp.concatenate([x_ref[...], b_ref[...], c_ref[...]], axis=1).astype(F32)
    ext = jnp.concatenate([tail_ref[...], cur], axis=0)
    tail_ref[...] = cur[lb - SUBLANES:lb, :]
    acc = cb_ref[...] + cw_ref[SSD_CONV - 1:SSD_CONV, :] * cur
    for j in range(SSD_CONV - 1):
        shifted = pltpu.roll(ext, SSD_CONV - 1 - j, 0)[SUBLANES:SUBLANES + lb, :]
        acc = acc + cw_ref[j:j + 1, :] * shifted
    xbc = _silu(acc)

    lane = lax.broadcasted_iota(jnp.int32, (1, LANES), 1)
    dt = _softplus(dt_ref[...].astype(F32) + dtb_ref[...])
    a_neg = jnp.where(lane < SSD_HEADS, -jnp.exp(alog_ref[...]), 0.0)
    ri = lax.broadcasted_iota(jnp.int32, (lb, lb), 0)
    ci = lax.broadcasted_iota(jnp.int32, (lb, lb), 1)
    tril = jnp.where((ri >= ci) & ((ri // L) == (ci // L)), 1.0, 0.0).astype(BF16)
    acum = _select_dot_lhs(tril, dt * a_neg, 3)
    wgt_small = jnp.concatenate([jnp.exp(acum[c * L + L - 1:c * L + L, :] - acum[crow(c), :]) for c in chunks], axis=0)
    acum_cat = jnp.concatenate(_split_bf16(acum, 3), axis=1)
    dt_cat = jnp.concatenate(_split_bf16(dt, 2), axis=1)
    eacum_cat = jnp.concatenate(_split_bf16(jnp.exp(acum), 2), axis=1)
    wgt_cat = jnp.concatenate(_split_bf16(wgt_small, 2), axis=1)
    acum_t = [jnp.concatenate([acum[crow(c), :], acum[crow(c), :]], axis=0).T for c in chunks]

    lane_lo = lane < hp
    row_l = lax.broadcasted_iota(jnp.int32, (L, gw), 0)
    src_s = lax.broadcasted_iota(jnp.int32, (L, gw), 1) & (hp - 1)
    causal = src_s <= row_l
    blockmask = bm_ref[...]
    heads_per_group = SSD_HEADS // SSD_GROUPS

    xs, b_bf, c_bf = [], [], []
    for g in groups:
        xs.append(xbc[:, g * gw:(g + 1) * gw])
        b_bf.append(xbc[:, SSD_D_INNER + g * ns:SSD_D_INNER + (g + 1) * ns])
        c_bf.append(xbc[:, SSD_D_INNER + SSD_BC_W + g * ns:SSD_D_INNER + SSD_BC_W + (g + 1) * ns].astype(BF16))
    b_t = [[b_bf[g][crow(c), :].T.astype(BF16) for g in groups] for c in chunks]
    b_bf = [b.astype(BF16) for b in b_bf]

    col, dt_exp, ecol, wgt = [], [], [], []
    for g in groups:
        e_g = e_ref[:, pl.ds(g * gw, gw)]
        col.append(_dot(acum_cat, e_g))
        dt_exp.append(_dot(dt_cat, e_g[:2 * LANES, :]))
        ecol.append(_dot(eacum_cat, e_g[:2 * LANES, :]))
        wgt.append(_dot(wgt_cat, e_g[:2 * LANES, :]))
    cb = [[_dot_nt(c_bf[g][crow(c), :], b_bf[g][crow(c), :]) for g in groups] for c in chunks]

    xdt = [xs[g] * dt_exp[g] for g in groups]
    xdt_bf = [t.astype(BF16) for t in xdt]
    xw = [(xdt[g] * wgt[g]).astype(BF16) for g in groups]
    m = []
    for c in chunks:
        m_c = []
        for g in groups:
            row_parts = []
            for pr in range(heads_per_group // 2):
                h0 = g * heads_per_group + 2 * pr
                r = jnp.where(lane_lo, acum_t[c][h0:h0 + 1, :], acum_t[c][h0 + 1:h0 + 2, :])
                row_parts.append(jnp.broadcast_to(r, (L, LANES)))
            row = jnp.concatenate(row_parts, axis=1)
            decay = jnp.where(causal, jnp.exp(col[g][crow(c), :] - row), 0.0)
            cb2 = jnp.concatenate([cb[c][g], cb[c][g]], axis=1)
            cb_exp = jnp.concatenate([cb2] * (gw // LANES), axis=1)
            m_c.append((cb_exp * decay).astype(BF16))
        m.append(m_c)

    y_diag, upd = [], []
    for c in chunks:
        yd_c, upd_c = [], []
        for g in groups:
            y_parts = []
            for qd in range(gw // MXU_TILE):
                sl = slice(qd * MXU_TILE, (qd + 1) * MXU_TILE)
                rhs = jnp.concatenate([xdt_bf[g][crow(c), sl]] * (MXU_TILE // hp), axis=0) * blockmask
                y_parts.append(_dot(m[c][g][:, sl], rhs))
            yd_c.append(jnp.concatenate(y_parts, axis=1))
            upd_c.append(_dot(b_t[c][g], xw[g][crow(c), :]))
        y_diag.append(yd_c)
        upd.append(upd_c)

    for g in groups:
        cols = pl.ds(g * gw, gw)
        state = st_ref[g]
        y_g = []
        for c in chunks:
            y_off = _dot(c_bf[g][crow(c), :], state.astype(BF16)) * ecol[g][crow(c), :]
            y_g.append(y_diag[c][g] + y_off)
            state = ecol[g][c * L + L - 1:c * L + L, :] * state + upd[c][g]
        st_ref[g] = state
        y = jnp.concatenate(y_g, axis=0) + xs[g] * dsk_ref[:, cols]
        y = y * _silu(z_ref[:, cols].astype(F32))
        o_ref[:, cols] = _rms_norm(y, ng_ref[:, cols]).astype(o_ref.dtype)


def _ssd(proj, conv_w, conv_b, dt_bias, a_log, d_skip, norm_g, batch, seq, n_chunks=2):
    t = batch * seq
    L = n_chunks * SSD_CHUNK
    ns = seq // L
    pad = LANES - SSD_HEADS
    expand = np.kron(np.eye(LANES, SSD_HEADS), np.ones((1, SSD_HEAD_DIM)))
    expand = jnp.asarray(np.tile(expand, (3, 1)), BF16)
    blockmask = jnp.asarray(np.kron(np.eye(MXU_TILE // SSD_HEAD_DIM), np.ones((SSD_HEAD_DIM, SSD_HEAD_DIM))), BF16)
    row = lambda b, n: b * ns + n
    const = lambda b, n: (0, 0)
    vmem = (2 * L * (SSD_CONV_CH + SSD_D_INNER + LANES) * 2 + 2 * L * SSD_D_INNER * 2
            + 2 * (6 * SSD_CONV_CH * 4 + 3 * LANES * SSD_D_INNER * 2)
            + SUBLANES * SSD_CONV_CH * 4 + SSD_GROUPS * SSD_D_STATE * SSD_GROUP_W * 4 + (12 << 20))
    return pl.pallas_call(
        functools.partial(_ssd_kernel, n_chunks=n_chunks),
        grid=(batch, ns),
        in_specs=[
            pl.BlockSpec((L, SSD_D_INNER), lambda b, n: (row(b, n), SSD_X // SSD_D_INNER)),
            pl.BlockSpec((L, SSD_BC_W), lambda b, n: (row(b, n), SSD_B // SSD_BC_W)),
            pl.BlockSpec((L, SSD_BC_W), lambda b, n: (row(b, n), SSD_C // SSD_BC_W)),
            pl.BlockSpec((L, SSD_D_INNER), lambda b, n: (row(b, n), SSD_Z // SSD_D_INNER)),
            pl.BlockSpec((L, LANES), lambda b, n: (row(b, n), SSD_DT // LANES)),
            pl.BlockSpec((SSD_CONV, SSD_CONV_CH), const),
            pl.BlockSpec((1, SSD_CONV_CH), const),
            pl.BlockSpec((1, LANES), const),
            pl.BlockSpec((1, LANES), const),
            pl.BlockSpec((1, SSD_D_INNER), const),
            pl.BlockSpec((1, SSD_D_INNER), const),
            pl.BlockSpec((3 * LANES, SSD_D_INNER), const),
            pl.BlockSpec((MXU_TILE, MXU_TILE), const),
        ],
        out_specs=pl.BlockSpec((L, SSD_D_INNER), lambda b, n: (row(b, n), 0)),
        out_shape=jax.ShapeDtypeStruct((t, SSD_D_INNER), BF16),
        scratch_shapes=[pltpu.VMEM((SUBLANES, SSD_CONV_CH), F32),
                        pltpu.VMEM((SSD_GROUPS, SSD_D_STATE, SSD_GROUP_W), F32)],
        compiler_params=_compiler_params(("parallel", "arbitrary"), vmem),
        name="ssd",
    )(proj, proj, proj, proj, proj, conv_w, conv_b.reshape(1, -1),
      jnp.pad(dt_bias, (0, pad)).reshape(1, LANES), jnp.pad(a_log, (0, pad)).reshape(1, LANES),
      jnp.repeat(d_skip, SSD_HEAD_DIM).reshape(1, -1), norm_g.reshape(1, -1), expand, blockmask)


def _hyb_weight(w_in):
    w = w_in.astype(BF16)
    glr_end = HYB_SQ + GLA_GATE_RANK
    zeros = jnp.zeros((w.shape[0], HYB_W - HYB_GLR - GLA_GATE_RANK), BF16)
    return jnp.concatenate([w[:, :HYB_SQ], w[:, glr_end:], w[:, HYB_SQ:glr_end], zeros], axis=1)


def _ssd_weight(w_in):
    return jnp.pad(w_in, ((0, 0), (0, SSD_W - w_in.shape[1]))).astype(BF16)


def _gla_swa_mixer(h, pos_col, norm_gain, w_in, gate_w2, gate_b, gla_norm, q_norm, k_norm, sinks, w_out, batch, seq):
    proj = _norm_matmul(h, norm_gain, _hyb_weight(w_in), tn=1536)
    w2p = jnp.pad(gate_w2, ((0, LANES - GLA_GATE_RANK), (0, 0))).astype(BF16)
    o_gla = _gla(proj, w2p, gate_b, gla_norm, batch, seq)
    o_swa = _swa(proj, pos_col, sinks, q_norm, k_norm, batch, seq)
    return _matmul_residual([o_gla, o_swa], w_out.astype(BF16), h)


def _ssd_mixer(h, norm_gain, w_in, conv_w, conv_b, dt_bias, a_log, d_skip, norm_g, w_out, batch, seq):
    proj = _norm_matmul(h, norm_gain, _ssd_weight(w_in), tn=1536)
    y = _ssd(proj, conv_w, conv_b, dt_bias, a_log, d_skip, norm_g, batch, seq)
    return _matmul_residual([y], w_out.astype(BF16), h)


def kernel(x, positions, norm_ffn, w_ffn_gu, w_ffn_down, norm_mix, hyb_w_in, gla_gate_w2, gla_gate_b, gla_norm,
           attn_q_norm, attn_k_norm, attn_sinks, hyb_w_out, ssd_w_in, ssd_conv_w, ssd_conv_b, ssd_dt_bias, ssd_a_log,
           ssd_d, ssd_norm, ssd_w_out):
    batch, seq, d = x.shape
    h = x.reshape(batch * seq, d)
    pos_col = positions.reshape(batch * seq, 1)
    depth = norm_ffn.shape[0]
    ffns = [(layer, pos) for layer in range(depth) for pos in range(2)]
    weights = (w_ffn_gu[0, 0].astype(BF16), w_ffn_down[0, 0].astype(BF16))
    for layer in range(depth):
        i = layer // 2
        for pos in range(2):
            if pos == 1:
                if layer % 2 == 0:
                    h = _gla_swa_mixer(h, pos_col, norm_mix[layer], hyb_w_in[i], gla_gate_w2[i], gla_gate_b[i],
                                       gla_norm[i], attn_q_norm[i], attn_k_norm[i], attn_sinks[i], hyb_w_out[i],
                                       batch, seq)
                else:
                    h = _ssd_mixer(h, norm_mix[layer], ssd_w_in[i], ssd_conv_w[i], ssd_conv_b[i], ssd_dt_bias[i],
                                   ssd_a_log[i], ssd_d[i], ssd_norm[i], ssd_w_out[i], batch, seq)
            k = ffns.index((layer, pos))
            nxt = (w_ffn_gu, w_ffn_down) + ffns[k + 1] if k + 1 < len(ffns) else None
            h, weights = _ffn(h, norm_ffn[layer, pos], weights[0], weights[1], nxt)
    return h.reshape(batch, seq, d)
```
